```python
import jax, jax.numpy as jnp
from jax import lax
import numpy as np

D_MODEL = 1024
BATCH = 16
SEQ = 2048
DEPTH = 2
DEC_BATCH = 32
DEC_SEQ = 64
PAST_LEN = 2048

N_HEADS = 16
HEAD_DIM = D_MODEL // N_HEADS
CHUNK = 64
LEFT_CHUNKS = 8
BAND = (LEFT_CHUNKS + 1) * CHUNK
A_CACHE_ROWS = LEFT_CHUNKS * CHUNK
REL_CLIP = 128
N_REL = 2 * REL_CLIP + 1
SB_BLOCK = 128
N_A_LAYERS = DEPTH // 2
N_B_LAYERS = DEPTH - N_A_LAYERS
RMS_EPS = 1e-6
NEG_INF = -1e30
SCALE = HEAD_DIM ** -0.5

kernel_name = 'yoco_chunkband_stickbreaking_step'


def rms_norm(x, g):
    xf = x.astype(jnp.float32)
    y = xf * lax.rsqrt(jnp.mean(xf * xf, axis=-1, keepdims=True) + RMS_EPS)
    return (y * g.astype(jnp.float32)).astype(x.dtype)


def split_heads(t):
    return t.reshape(*t.shape[:-1], N_HEADS, HEAD_DIM)


def gated_out(o, gate, w_out):
    b, t = o.shape[:2]
    return (o.reshape(b, t, D_MODEL) * jax.nn.silu(gate)) @ w_out


def a_project(x, g, w_in):
    h = rms_norm(x, g)
    q, k, v, gate = jnp.split(h @ w_in, 4, axis=-1)
    return split_heads(q), split_heads(k), split_heads(v), gate


def b_project(x, g, w_in):
    h = rms_norm(x, g)
    q, gate = jnp.split(h @ w_in, 2, axis=-1)
    return split_heads(q), gate


def shared_kv(x, g, w_kv):
    h = rms_norm(x, g)
    k, v = jnp.split(h @ w_kv, 2, axis=-1)
    return split_heads(k), split_heads(v)


def band_attend(q, k, v, q_pos, k_pos, rel_bias):
    s = jnp.einsum('bqhd,bkhd->bhqk', q, k).astype(jnp.float32) * SCALE
    rel = jnp.clip(q_pos[:, None] - k_pos[None, :], -REL_CLIP, REL_CLIP) + REL_CLIP
    s = s + rel_bias[:, rel].astype(jnp.float32)[None]
    qc = q_pos // CHUNK
    kc = k_pos // CHUNK
    mask = ((k_pos[None, :] >= 0) & (kc[None, :] <= qc[:, None])
            & (kc[None, :] >= qc[:, None] - LEFT_CHUNKS))
    s = jnp.where(mask[None, None], s, NEG_INF)
    p = jax.nn.softmax(s, axis=-1).astype(v.dtype)
    return jnp.einsum('bhqk,bkhd->bqhd', p, v)


def chunk_band_prompt(q, k, v, rel_bias):
    b, t, h, dh = q.shape
    n_chunks = t // CHUNK
    pad = A_CACHE_ROWS
    kp = jnp.pad(k, ((0, 0), (pad, 0), (0, 0), (0, 0)))
    vp = jnp.pad(v, ((0, 0), (pad, 0), (0, 0), (0, 0)))
    qc = q.reshape(b, n_chunks, CHUNK, h, dh).transpose(1, 0, 2, 3, 4)

    def one_chunk(args):
        c, qb = args
        start = c * CHUNK
        kb = lax.dynamic_slice_in_dim(kp, start, BAND, axis=1)
        vb = lax.dynamic_slice_in_dim(vp, start, BAND, axis=1)
        q_pos = start + jnp.arange(CHUNK, dtype=jnp.int32)
        k_pos = start - pad + jnp.arange(BAND, dtype=jnp.int32)
        return band_attend(qb, kb, vb, q_pos, k_pos, rel_bias)

    out = lax.map(one_chunk, (jnp.arange(n_chunks, dtype=jnp.int32), qc))
    return out.transpose(1, 0, 2, 3, 4).reshape(b, t, h, dh)


def sb_block(q, k, v, q_pos, k_pos):
    z = jnp.einsum('bqhd,bkhd->bhqk', q, k).astype(jnp.float32) * SCALE
    mask = (k_pos[None, :] < q_pos[:, None])[None, None]
    log_fail = jnp.where(mask, jax.nn.log_sigmoid(-z), 0.0)
    later = jnp.flip(jnp.cumsum(jnp.flip(log_fail, -1), axis=-1), -1) - log_fail
    w = jnp.where(mask, jnp.exp(jax.nn.log_sigmoid(z) + later), 0.0)
    return jnp.einsum('bhqk,bkhd->bqhd', w.astype(v.dtype), v)


def sb_prompt(q, k, v):
    t = q.shape[1]
    pos = jnp.arange(t, dtype=jnp.int32)
    outs = []
    for start in range(0, t, SB_BLOCK):
        end = start + SB_BLOCK
        outs.append(sb_block(q[:, start:end], k[:, :end], v[:, :end], pos[start:end], pos[:end]))
    return jnp.concatenate(outs, axis=1)


def setup_inputs(seed: int = 0) -> dict:
    key = jax.random.key(seed)
    ks = jax.random.split(key, 16)
    d = D_MODEL
    a_rows = min(A_CACHE_ROWS, PAST_LEN)
    nrm = jax.random.normal
    return {
        'x_prompt': nrm(ks[0], (BATCH, SEQ, d), jnp.float32),
        'x_sample': nrm(ks[1], (DEC_BATCH, DEC_SEQ, d), jnp.float32),
        'cache_a_k': nrm(ks[2], (N_A_LAYERS, DEC_BATCH, a_rows, N_HEADS, HEAD_DIM), jnp.float32),
        'cache_a_v': nrm(ks[3], (N_A_LAYERS, DEC_BATCH, a_rows, N_HEADS, HEAD_DIM), jnp.float32),
        'cache_b_k': nrm(ks[4], (DEC_BATCH, PAST_LEN, N_HEADS, HEAD_DIM), jnp.float32),
        'cache_b_v': nrm(ks[5], (DEC_BATCH, PAST_LEN, N_HEADS, HEAD_DIM), jnp.float32),
        'norm_a': 1.0 + 0.02 * nrm(ks[6], (N_A_LAYERS, d), jnp.float32),
        'w_in_a': nrm(ks[7], (N_A_LAYERS, d, 4 * d), jnp.float32) * d ** -0.5,
        'rel_bias_a': 0.1 * nrm(ks[8], (N_A_LAYERS, N_HEADS, N_REL), jnp.float32),
        'w_out_a': nrm(ks[9], (N_A_LAYERS, d, d), jnp.float32) * d ** -0.5,
        'norm_kv': 1.0 + 0.02 * nrm(ks[10], (d,), jnp.float32),
        'w_kv': nrm(ks[11], (d, 2 * d), jnp.float32) * d ** -0.5,
        'norm_b': 1.0 + 0.02 * nrm(ks[12], (N_B_LAYERS, d), jnp.float32),
        'w_in_b': nrm(ks[13], (N_B_LAYERS, d, 2 * d), jnp.float32) * d ** -0.5,
        'w_out_b': nrm(ks[14], (N_B_LAYERS, d, d), jnp.float32) * d ** -0.5,
        'norm_f': 1.0 + 0.02 * nrm(ks[15], (d,), jnp.float32),
    }


def reference(x_prompt, x_sample, cache_a_k, cache_a_v, cache_b_k, cache_b_v,
              norm_a, w_in_a, rel_bias_a, w_out_a, norm_kv, w_kv,
              norm_b, w_in_b, w_out_b, norm_f):
    past_len = cache_b_k.shape[1]
    ts = x_sample.shape[1]
    a_rows = cache_a_k.shape[2]
    q_pos_s = past_len + jnp.arange(ts, dtype=jnp.int32)
    k_pos_a = past_len - a_rows + jnp.arange(a_rows + ts, dtype=jnp.int32)
    k_pos_b = jnp.arange(past_len + ts, dtype=jnp.int32)

    xp, xs = x_prompt, x_sample
    a_kp, a_vp, a_ks, a_vs = [], [], [], []
    for layer in range(DEPTH):
        if layer < N_A_LAYERS:
            i = layer
            qp, kp, vp, gp = a_project(xp, norm_a[i], w_in_a[i])
            qs, ks, vs, gs = a_project(xs, norm_a[i], w_in_a[i])
            op = chunk_band_prompt(qp, kp, vp, rel_bias_a[i])
            k_all = jnp.concatenate([cache_a_k[i], ks], axis=1)
            v_all = jnp.concatenate([cache_a_v[i], vs], axis=1)
            o_s = band_attend(qs, k_all, v_all, q_pos_s, k_pos_a, rel_bias_a[i])
            xp = xp + gated_out(op, gp, w_out_a[i])
            xs = xs + gated_out(o_s, gs, w_out_a[i])
            keep = min(A_CACHE_ROWS, xp.shape[1])
            a_kp.append(kp[:, -keep:])
            a_vp.append(vp[:, -keep:])
            a_ks.append(ks)
            a_vs.append(vs)
        else:
            if layer == N_A_LAYERS:
                kb_p, vb_p = shared_kv(xp, norm_kv, w_kv)
                kb_s, vb_s = shared_kv(xs, norm_kv, w_kv)
                kb_all = jnp.concatenate([cache_b_k, kb_s], axis=1)
                vb_all = jnp.concatenate([cache_b_v, vb_s], axis=1)
            j = layer - N_A_LAYERS
            qp, gp = b_project(xp, norm_b[j], w_in_b[j])
            qs, gs = b_project(xs, norm_b[j], w_in_b[j])
            op = sb_prompt(qp, kb_p, vb_p)
            o_s = sb_block(qs, kb_all, vb_all, q_pos_s, k_pos_b)
            xp = xp + gated_out(op, gp, w_out_b[j])
            xs = xs + gated_out(o_s, gs, w_out_b[j])

    y_prompt = rms_norm(xp, norm_f)
    y_sample = rms_norm(xs, norm_f)
    new_a_k_prompt = jnp.stack(a_kp)
    new_a_v_prompt = jnp.stack(a_vp)
    new_a_k_sample = jnp.stack(a_ks)
    new_a_v_sample = jnp.stack(a_vs)
    return (y_prompt, y_sample, new_a_k_prompt, new_a_v_prompt, kb_p, vb_p,
            new_a_k_sample, new_a_v_sample, kb_s, vb_s)
```

```python
import functools
from typing import NamedTuple, Optional

import jax
import jax.numpy as jnp
from jax import lax
from jax.experimental import pallas as pl
from jax.experimental.pallas import tpu as pltpu

D_MODEL = 1024
N_HEADS = 16
HEAD_DIM = 64
PAIR_W = 2 * HEAD_DIM
N_PAIRS = N_HEADS // 2
CHUNK = 64
LEFT_CHUNKS = 8
A_CACHE_ROWS = LEFT_CHUNKS * CHUNK
REL_CLIP = 128
N_REL = 2 * REL_CLIP + 1
N_REL_PAD = 384
RMS_EPS = 1e-6
NEG_INF = -1e30
SCALE = HEAD_DIM ** -0.5

BAND_TQ = 128
BAND_TK = A_CACHE_ROWS + BAND_TQ
BAND_VARIANTS = A_CACHE_ROWS // BAND_TQ + 1
SB_TQ = 256
SB_TK = 256

VMEM_LIMIT = 56 * 1024 * 1024

F32 = jnp.float32
BF16 = jnp.bfloat16


def _params(sem):
    return pltpu.CompilerParams(dimension_semantics=sem, vmem_limit_bytes=VMEM_LIMIT)


class Seg(NamedTuple):
    norm: int
    w: int
    col: int
    scale: float
    pm: Optional[int]
    f32: Optional[int]


def _dense_kernel(*refs, has_resid, emit_x, n_norm, n_w, plan, final_norm, n_pm, n_f32):
    it = iter(refs)
    x_ref = next(it)
    og_ref = next(it) if has_resid else None
    wo_ref = next(it) if has_resid else None
    g_ref = next(it) if n_norm else None
    w_refs = [next(it) for _ in range(n_w)]
    gf_ref = next(it) if final_norm else None
    xo_ref = next(it) if emit_x else None
    pm_refs = [next(it) for _ in range(n_pm)]
    f32_refs = [next(it) for _ in range(n_f32)]
    y_ref = next(it) if final_norm else None

    x = x_ref[0]
    if has_resid:
        og = jnp.concatenate([og_ref[0, p] for p in range(N_PAIRS)], axis=1)
        x = x + jnp.dot(og, wo_ref[...], preferred_element_type=F32)
        if emit_x:
            xo_ref[0] = x
    if n_norm or final_norm:
        xn = x * lax.rsqrt(jnp.mean(x * x, axis=-1, keepdims=True) + RMS_EPS)
    if final_norm:
        y_ref[0] = xn * gf_ref[...]
    hs = [(xn * g_ref[i:i + 1, :]).astype(BF16) for i in range(n_norm)]
    for seg in plan:
        w = w_refs[seg.w][:, seg.col * D_MODEL:(seg.col + 1) * D_MODEL]
        acc = jnp.dot(hs[seg.norm], w, preferred_element_type=F32)
        if seg.f32 is not None:
            f32_refs[seg.f32][0] = acc
        if seg.pm is not None:
            ab = (acc * seg.scale).astype(BF16) if seg.scale != 1.0 else acc.astype(BF16)
            for p in range(N_PAIRS):
                pm_refs[seg.pm][0, p] = ab[:, p * PAIR_W:(p + 1) * PAIR_W]


def _dense_call(x, *, og=None, wo=None, gains=None, weights=(), plan=(), gf=None,
                emit_x=False, f32_last_rows=None, tm=512, name="dense"):
    bx, tx, d = x.shape
    assert d == D_MODEL and tx % tm == 0
    has_resid = og is not None
    n_norm = 0 if gains is None else gains.shape[0]
    n_pm = sum(s.pm is not None for s in plan)
    n_f32 = sum(s.f32 is not None for s in plan)
    final_norm = gf is not None

    row_spec = pl.BlockSpec((1, tm, d), lambda b, t: (b, t, 0))
    pm_spec = pl.BlockSpec((1, N_PAIRS, tm, PAIR_W), lambda b, t: (b, 0, t, 0))
    whole = lambda a: pl.BlockSpec(a.shape, lambda b, t: (0,) * a.ndim)

    in_arrays, in_specs = [x], [row_spec]
    if has_resid:
        in_arrays += [og, wo]
        in_specs += [pm_spec, whole(wo)]
    if n_norm:
        in_arrays.append(gains)
        in_specs.append(whole(gains))
    for w in weights:
        in_arrays.append(w)
        in_specs.append(whole(w))
    if final_norm:
        in_arrays.append(gf)
        in_specs.append(whole(gf))

    out_shapes, out_specs = [], []
    if emit_x:
        out_shapes.append(jax.ShapeDtypeStruct((bx, tx, d), F32))
        out_specs.append(row_spec)
    for _ in range(n_pm):
        out_shapes.append(jax.ShapeDtypeStruct((bx, N_PAIRS, tx, PAIR_W), BF16))
        out_specs.append(pm_spec)
    for _ in range(n_f32):
        if f32_last_rows is None:
            out_shapes.append(jax.ShapeDtypeStruct((bx, tx, d), F32))
            out_specs.append(row_spec)
        else:
            assert f32_last_rows == tm
            out_shapes.append(jax.ShapeDtypeStruct((bx, tm, d), F32))
            out_specs.append(pl.BlockSpec((1, tm, d), lambda b, t: (b, 0, 0)))
    if final_norm:
        out_shapes.append(jax.ShapeDtypeStruct((bx, tx, d), F32))
        out_specs.append(row_spec)

    body = functools.partial(
        _dense_kernel, has_resid=has_resid, emit_x=emit_x, n_norm=n_norm, n_w=len(weights),
        plan=tuple(plan), final_norm=final_norm, n_pm=n_pm, n_f32=n_f32)
    return pl.pallas_call(
        body, grid=(bx, tx // tm), in_specs=in_specs, out_specs=out_specs, out_shape=out_shapes,
        name=name, compiler_params=_params(("arbitrary", "arbitrary")))(*in_arrays)


def _bias_kernel(rb_ref, bp_ref, bs_ref, *, past_len, ts):
    width = 768
    d0 = A_CACHE_ROWS
    rb = rb_ref[0]
    hi = rb.astype(BF16)
    r1 = rb - hi.astype(F32)
    mid = r1.astype(BF16)
    lo = (r1 - mid.astype(F32)).astype(BF16)
    c = lax.broadcasted_iota(jnp.int32, (N_REL_PAD, width), 0)
    n = lax.broadcasted_iota(jnp.int32, (N_REL_PAD, width), 1)
    m = jnp.where(n < BAND_TK, n, n - width)
    tgt = jnp.clip(d0 - m, -REL_CLIP, REL_CLIP) + REL_CLIP
    onehot = jnp.where(c == tgt, 1.0, 0.0).astype(BF16)
    r_ext = (jnp.dot(hi, onehot, preferred_element_type=F32)
             + jnp.dot(mid, onehot, preferred_element_type=F32)
             + jnp.dot(lo, onehot, preferred_element_type=F32))

    sub = lax.broadcasted_iota(jnp.int32, (8, width), 0)
    nks = A_CACHE_ROWS + ts
    for h2 in range(2):
        base = jnp.broadcast_to(r_ext[h2:h2 + 1, :], (8, width))
        b8 = base
        for r in range(1, 8):
            b8 = jnp.where(sub == r, pltpu.roll(base, r, 1), b8)
        blocks = [b8] + [pltpu.roll(b8, 8 * gi, 1) for gi in range(1, BAND_TQ // 8)]
        toep = jnp.concatenate(blocks, axis=0)[:, :BAND_TK]

        def masked(q0, k0, rows, cols):
            qpos = q0 + lax.broadcasted_iota(jnp.int32, (rows, cols), 0)
            kpos = k0 + lax.broadcasted_iota(jnp.int32, (rows, cols), 1)
            qc0 = qpos - jnp.bitwise_and(qpos, CHUNK - 1)
            lo_k = jnp.maximum(qc0 - A_CACHE_ROWS, 0)
            ok = jnp.logical_and(kpos >= lo_k, kpos < qc0 + CHUNK)
            return jnp.where(ok, toep[:rows, :cols], NEG_INF)

        for v in range(BAND_VARIANTS):
            q0 = BAND_TQ * v
            bp_ref[0, v, h2 * BAND_TQ:(h2 + 1) * BAND_TQ, :] = masked(q0, q0 - A_CACHE_ROWS, BAND_TQ, BAND_TK)
        bs_ref[0, h2 * ts:(h2 + 1) * ts, :] = masked(past_len, past_len - A_CACHE_ROWS, ts, nks)


def _bias_call(rel_bias, past_len, ts):
    rb = jnp.pad(rel_bias.reshape(N_PAIRS, 2, N_REL), ((0, 0), (0, 6), (0, N_REL_PAD - N_REL)))
    nks = A_CACHE_ROWS + ts
    return pl.pallas_call(
        functools.partial(_bias_kernel, past_len=past_len, ts=ts),
        grid=(N_PAIRS,),
        in_specs=[pl.BlockSpec((1, 8, N_REL_PAD), lambda p: (p, 0, 0))],
        out_specs=[pl.BlockSpec((1, BAND_VARIANTS, 2 * BAND_TQ, BAND_TK), lambda p: (p, 0, 0, 0)),
                   pl.BlockSpec((1, 2 * ts, nks), lambda p: (p, 0, 0))],
        out_shape=[jax.ShapeDtypeStruct((N_PAIRS, BAND_VARIANTS, 2 * BAND_TQ, BAND_TK), F32),
                   jax.ShapeDtypeStruct((N_PAIRS, 2 * ts, nks), F32)],
        name="band_bias", compiler_params=_params(("arbitrary",)))(rb)


def _first_head_lanes(rows):
    return lax.broadcasted_iota(jnp.int32, (rows, PAIR_W), 1) < HEAD_DIM


def _stack_heads(q, first):
    zero = jnp.zeros_like(q)
    return jnp.concatenate([jnp.where(first, q, zero), jnp.where(first, zero, q)], axis=0)


def _unstack_heads(o2, first):
    rows = o2.shape[0] // 2
    return jnp.where(first, o2[:rows], o2[rows:])


def _nt_dot(a, b):
    return lax.dot_general(a, b, (((1,), (1,)), ((), ())), preferred_element_type=F32)


def _gate(o, g):
    g = g.astype(F32)
    return (o * (g * jax.nn.sigmoid(g))).astype(BF16)


def _softmax_pv(s, v):
    m = jnp.max(s, axis=-1, keepdims=True)
    p = jnp.exp(s - m)
    l = jnp.sum(p, axis=-1, keepdims=True)
    o = jnp.dot(p.astype(BF16), v, preferred_element_type=F32)
    return o * (1.0 / l)


def _band_prompt_kernel(q_ref, k_ref, v_ref, g_ref, bias_ref, o_ref, kpad, vpad, *, t):
    zeros = jnp.zeros((A_CACHE_ROWS, PAIR_W), BF16)
    kpad[0:A_CACHE_ROWS, :] = zeros
    vpad[0:A_CACHE_ROWS, :] = zeros
    kpad[A_CACHE_ROWS:, :] = k_ref[0, 0]
    vpad[A_CACHE_ROWS:, :] = v_ref[0, 0]
    first = _first_head_lanes(BAND_TQ)

    def body(qt, carry):
        r0 = pl.multiple_of(qt * BAND_TQ, BAND_TQ)
        q2 = _stack_heads(q_ref[0, 0, pl.ds(r0, BAND_TQ), :], first)
        kw = kpad[pl.ds(r0, BAND_TK), :]
        vw = vpad[pl.ds(r0, BAND_TK), :]
        s = _nt_dot(q2, kw) + bias_ref[0, jnp.minimum(qt, BAND_VARIANTS - 1)]
        o = _unstack_heads(_softmax_pv(s, vw), first)
        o_ref[0, 0, pl.ds(r0, BAND_TQ), :] = _gate(o, g_ref[0, 0, pl.ds(r0, BAND_TQ), :])
        return carry

    lax.fori_loop(0, t // BAND_TQ, body, 0)


def _band_prompt_call(q, k, v, g, bias_p):
    b, _, t, _ = q.shape
    blk = pl.BlockSpec((1, 1, t, PAIR_W), lambda p, bb: (bb, p, 0, 0))
    return pl.pallas_call(
        functools.partial(_band_prompt_kernel, t=t),
        grid=(N_PAIRS, b),
        in_specs=[blk, blk, blk, blk,
                  pl.BlockSpec((1, BAND_VARIANTS, 2 * BAND_TQ, BAND_TK), lambda p, bb: (p, 0, 0, 0))],
        out_specs=blk,
        out_shape=jax.ShapeDtypeStruct(q.shape, BF16),
        scratch_shapes=[pltpu.VMEM((t + A_CACHE_ROWS, PAIR_W), BF16),
                        pltpu.VMEM((t + A_CACHE_ROWS, PAIR_W), BF16)],
        name="band_prompt", compiler_params=_params(("arbitrary", "arbitrary")))(q, k, v, g, bias_p)


def _band_sample_kernel(q_ref, k_ref, v_ref, g_ref, ck_ref, cv_ref, bias_ref, o_ref, *, ts):
    first = _first_head_lanes(ts)
    for p in range(N_PAIRS):
        lanes = slice(p * PAIR_W, (p + 1) * PAIR_W)
        q2 = _stack_heads(q_ref[0, p], first)
        kk = jnp.concatenate([ck_ref[0, :, lanes].astype(BF16), k_ref[0, p]], axis=0)
        vv = jnp.concatenate([cv_ref[0, :, lanes].astype(BF16), v_ref[0, p]], axis=0)
        s = _nt_dot(q2, kk) + bias_ref[p]
        o = _unstack_heads(_softmax_pv(s, vv), first)
        o_ref[0, p] = _gate(o, g_ref[0, p])


def _band_sample_call(q, k, v, g, cache_k, cache_v, bias_s, bs, ts):
    blk = pl.BlockSpec((1, N_PAIRS, ts, PAIR_W), lambda b: (0, 0, b, 0))
    cblk = pl.BlockSpec((1, A_CACHE_ROWS, D_MODEL), lambda b: (b, 0, 0))
    return pl.pallas_call(
        functools.partial(_band_sample_kernel, ts=ts),
        grid=(bs,),
        in_specs=[blk, blk, blk, blk, cblk, cblk,
                  pl.BlockSpec(bias_s.shape, lambda b: (0, 0, 0))],
        out_specs=blk,
        out_shape=jax.ShapeDtypeStruct(q.shape, BF16),
        name="band_sample", compiler_params=_params(("arbitrary",)))(q, k, v, g, cache_k, cache_v, bias_s)


def _suffix_matrix(n):
    r = lax.broadcasted_iota(jnp.int32, (n, n), 0)
    c = lax.broadcasted_iota(jnp.int32, (n, n), 1)
    return jnp.where(r > c, 1.0, 0.0).astype(BF16)


def _sb_tile(q2, kt, vt, suffix, carry, acc, causal):
    z = _nt_dot(q2, kt)
    softplus = jnp.maximum(z, 0.0) + jnp.log(1.0 + jnp.exp(-jnp.abs(z)))
    log_fail = -softplus
    if causal is not None:
        log_fail = jnp.where(causal, log_fail, 0.0)
    hi = log_fail.astype(BF16)
    lo = (log_fail - hi.astype(F32)).astype(BF16)
    later = (jnp.dot(hi, suffix, preferred_element_type=F32)
             + jnp.dot(lo, suffix, preferred_element_type=F32))
    w = jnp.exp((z - softplus) + later + carry)
    if causal is not None:
        w = jnp.where(causal, w, 0.0)
    acc = acc + jnp.dot(w.astype(BF16), vt, preferred_element_type=F32)
    carry = carry + jnp.sum(log_fail, axis=-1, keepdims=True)
    return carry, acc


def _sb_prompt_kernel(q_ref, k_ref, v_ref, g_ref, o_ref, *, t):
    first = _first_head_lanes(SB_TQ)
    suffix = _suffix_matrix(SB_TK)
    row = lax.broadcasted_iota(jnp.int32, (2 * SB_TQ, SB_TK), 0)
    col = lax.broadcasted_iota(jnp.int32, (2 * SB_TQ, SB_TK), 1)
    causal = col < jnp.bitwise_and(row, SB_TQ - 1)

    def q_body(qt, c0):
        r0 = pl.multiple_of(qt * SB_TQ, SB_TQ)
        q2 = _stack_heads(q_ref[0, 0, pl.ds(r0, SB_TQ), :], first)
        carry = jnp.zeros((2 * SB_TQ, 1), F32)
        acc = jnp.zeros((2 * SB_TQ, PAIR_W), F32)
        carry, acc = _sb_tile(q2, k_ref[0, 0, pl.ds(r0, SB_TK), :], v_ref[0, 0, pl.ds(r0, SB_TK), :],
                              suffix, carry, acc, causal)

        def k_body(i, c):
            k0 = pl.multiple_of((qt - 1 - i) * SB_TK, SB_TK)
            return _sb_tile(q2, k_ref[0, 0, pl.ds(k0, SB_TK), :], v_ref[0, 0, pl.ds(k0, SB_TK), :],
                            suffix, c[0], c[1], None)

        carry, acc = lax.fori_loop(0, qt, k_body, (carry, acc))
        o = _unstack_heads(acc, first)
        o_ref[0, 0, pl.ds(r0, SB_TQ), :] = _gate(o, g_ref[0, 0, pl.ds(r0, SB_TQ), :])
        return c0

    lax.fori_loop(0, t // SB_TQ, q_body, 0)


def _sb_prompt_call(q, k, v, g):
    b, _, t, _ = q.shape
    blk = pl.BlockSpec((1, 1, t, PAIR_W), lambda bb, p: (bb, p, 0, 0))
    return pl.pallas_call(
        functools.partial(_sb_prompt_kernel, t=t),
        grid=(b, N_PAIRS),
        in_specs=[blk, blk, blk, blk],
        out_specs=blk,
        out_shape=jax.ShapeDtypeStruct(q.shape, BF16),
        name="sb_prompt", compiler_params=_params(("arbitrary", "arbitrary")))(q, k, v, g)


def _sb_sample_kernel(q_ref, k_ref, v_ref, g_ref, ck_ref, cv_ref, o_ref, carry_ref, acc_ref, *, ts, n_kt):
    step = pl.program_id(1)
    first = _first_head_lanes(ts)

    @pl.when(step == 0)
    def _():
        suffix = _suffix_matrix(ts)
        row = lax.broadcasted_iota(jnp.int32, (2 * ts, ts), 0)
        col = lax.broadcasted_iota(jnp.int32, (2 * ts, ts), 1)
        causal = col < jnp.bitwise_and(row, ts - 1)
        for p in range(N_PAIRS):
            q2 = _stack_heads(q_ref[0, p], first)
            carry, acc = _sb_tile(q2, k_ref[0, p], v_ref[0, p], suffix,
                                  jnp.zeros((2 * ts, 1), F32), jnp.zeros((2 * ts, PAIR_W), F32), causal)
            carry_ref[p] = carry
            acc_ref[p] = acc

    @pl.when(step > 0)
    def _():
        suffix = _suffix_matrix(SB_TK)
        for p in range(N_PAIRS):
            lanes = slice(p * PAIR_W, (p + 1) * PAIR_W)
            q2 = _stack_heads(q_ref[0, p], first)
            carry, acc = _sb_tile(q2, ck_ref[0, :, lanes].astype(BF16), cv_ref[0, :, lanes].astype(BF16),
                                  suffix, carry_ref[p], acc_ref[p], None)
            carry_ref[p] = carry
            acc_ref[p] = acc

    @pl.when(step == n_kt)
    def _():
        for p in range(N_PAIRS):
            o_ref[0, p] = _gate(_unstack_heads(acc_ref[p], first), g_ref[0, p])


def _sb_sample_call(q, k, v, g, cache_k, cache_v, bs, ts):
    past = cache_k.shape[1]
    assert past % SB_TK == 0
    n_kt = past // SB_TK
    blk = pl.BlockSpec((1, N_PAIRS, ts, PAIR_W), lambda b, s: (0, 0, b, 0))
    cblk = pl.BlockSpec((1, SB_TK, D_MODEL), lambda b, s: (b, n_kt - jnp.maximum(s, 1), 0))
    return pl.pallas_call(
        functools.partial(_sb_sample_kernel, ts=ts, n_kt=n_kt),
        grid=(bs, n_kt + 1),
        in_specs=[blk, blk, blk, blk, cblk, cblk],
        out_specs=blk,
        out_shape=jax.ShapeDtypeStruct(q.shape, BF16),
        scratch_shapes=[pltpu.VMEM((N_PAIRS, 2 * ts, 1), F32),
                        pltpu.VMEM((N_PAIRS, 2 * ts, PAIR_W), F32)],
        name="sb_sample", compiler_params=_params(("arbitrary", "arbitrary")))(q, k, v, g, cache_k, cache_v)


def kernel(x_prompt, x_sample, cache_a_k, cache_a_v, cache_b_k, cache_b_v, norm_a, w_in_a, rel_bias_a,
           w_out_a, norm_kv, w_kv, norm_b, w_in_b, w_out_b, norm_f):
    b, t, d = x_prompt.shape
    bs, ts, _ = x_sample.shape
    past = cache_b_k.shape[1]
    assert d == D_MODEL and norm_a.shape[0] == 1 and norm_b.shape[0] == 1
    assert cache_a_k.shape[2] == A_CACHE_ROWS and t % SB_TQ == 0 and t >= A_CACHE_ROWS
    assert past % CHUNK == 0 and ts == CHUNK

    w_a = w_in_a[0].astype(BF16)
    wo_a = w_out_a[0].astype(BF16)
    w_kvb = w_kv.astype(BF16)
    w_b = w_in_b[0].astype(BF16)
    wo_b = w_out_b[0].astype(BF16)
    g_a = norm_a
    g_b = jnp.stack([norm_kv, norm_b[0]])
    g_f = norm_f[None]
    xs = x_sample.reshape(1, bs * ts, d)

    bias_p, bias_s = _bias_call(rel_bias_a[0], past, ts)

    plan_a = (Seg(0, 0, 0, SCALE, 0, None), Seg(0, 0, 1, 1.0, 1, 0),
              Seg(0, 0, 2, 1.0, 2, 1), Seg(0, 0, 3, 1.0, 3, None))
    plan_b = (Seg(0, 0, 0, 1.0, 0, 0), Seg(0, 0, 1, 1.0, 1, 1),
              Seg(1, 1, 0, SCALE, 2, None), Seg(1, 1, 1, 1.0, 3, None))

    qp, kp, vp, gp, akp, avp = _dense_call(x_prompt, gains=g_a, weights=(w_a,), plan=plan_a,
                                           f32_last_rows=A_CACHE_ROWS, name="proj_a_prompt")
    qs, ks, vs, gs, aks, avs = _dense_call(xs, gains=g_a, weights=(w_a,), plan=plan_a,
                                           name="proj_a_sample")

    ogp = _band_prompt_call(qp, kp, vp, gp, bias_p)
    ogs = _band_sample_call(qs, ks, vs, gs, cache_a_k[0].reshape(bs, A_CACHE_ROWS, d),
                            cache_a_v[0].reshape(bs, A_CACHE_ROWS, d), bias_s, bs, ts)

    xp1, kbp, vbp, qbp, gbp, kbp32, vbp32 = _dense_call(
        x_prompt, og=ogp, wo=wo_a, gains=g_b, weights=(w_kvb, w_b), plan=plan_b, emit_x=True, tm=256,
        name="out_a_proj_b_prompt")
    xs1, kbs, vbs, qbs, gbs, kbs32, vbs32 = _dense_call(
        xs, og=ogs, wo=wo_a, gains=g_b, weights=(w_kvb, w_b), plan=plan_b, emit_x=True, tm=256,
        name="out_a_proj_b_sample")

    obp = _sb_prompt_call(qbp, kbp, vbp, gbp)
    obs = _sb_sample_call(qbs, kbs, vbs, gbs, cache_b_k.reshape(bs, past, d),
                          cache_b_v.reshape(bs, past, d), bs, ts)

    (y_prompt,) = _dense_call(xp1, og=obp, wo=wo_b, gf=g_f, name="out_b_prompt")
    (y_sample,) = _dense_call(xs1, og=obs, wo=wo_b, gf=g_f, name="out_b_sample")

    heads = lambda a, n, rows: a.reshape(n, rows, N_HEADS, HEAD_DIM)
    return (y_prompt, y_sample.reshape(bs, ts, d),
            heads(akp, b, A_CACHE_ROWS)[None], heads(avp, b, A_CACHE_ROWS)[None],
            heads(kbp32, b, t), heads(vbp32, b, t),
            heads(aks, bs, ts)[None], heads(avs, bs, ts)[None],
            heads(kbs32, bs, ts), heads(vbs32, bs, ts))
```

```python
import functools
from typing import NamedTuple, Optional

import jax
import jax.numpy as jnp
from jax import lax
from jax.experimental import pallas as pl
from jax.experimental.pallas import tpu as pltpu

D_MODEL = 1024
N_HEADS = 16
HEAD_DIM = 64
PAIR_W = 2 * HEAD_DIM
N_PAIRS = N_HEADS // 2
CHUNK = 64
LEFT_CHUNKS = 8
A_CACHE_ROWS = LEFT_CHUNKS * CHUNK
REL_CLIP = 128
N_REL = 2 * REL_CLIP + 1
N_REL_PAD = 384
RMS_EPS = 1e-6
NEG_INF = -1e30
SCALE = HEAD_DIM ** -0.5
LOG2E = 1.4426950408889634

BAND_TQ = 128
BAND_TK = A_CACHE_ROWS + BAND_TQ
BAND_VARIANTS = A_CACHE_ROWS // BAND_TQ + 1
BAND_CHAINS = 4
SB_TQ = 256
SB_TK = 256
SB_CHAINS = 4
SB_ROW_BLOCK = 32

VMEM_LIMIT = 56 * 1024 * 1024

F32 = jnp.float32
BF16 = jnp.bfloat16


def _params(sem):
    return pltpu.CompilerParams(dimension_semantics=sem, vmem_limit_bytes=VMEM_LIMIT)


class Seg(NamedTuple):
    norm: int
    w: int
    col: int
    scale: float
    pm: Optional[int]
    f32: Optional[int]


def _dense_kernel(*refs, has_resid, emit_x, n_norm, n_w, plan, final_norm, n_pm, n_f32):
    it = iter(refs)
    x_ref = next(it)
    og_ref = next(it) if has_resid else None
    wo_ref = next(it) if has_resid else None
    g_ref = next(it) if n_norm else None
    w_refs = [next(it) for _ in range(n_w)]
    gf_ref = next(it) if final_norm else None
    xo_ref = next(it) if emit_x else None
    pm_refs = [next(it) for _ in range(n_pm)]
    f32_refs = [next(it) for _ in range(n_f32)]
    y_ref = next(it) if final_norm else None

    x = x_ref[0]
    if has_resid:
        og = jnp.concatenate([og_ref[0, p] for p in range(N_PAIRS)], axis=1)
        x = x + jnp.dot(og, wo_ref[...], preferred_element_type=F32)
        if emit_x:
            xo_ref[0] = x
    if n_norm or final_norm:
        xn = x * lax.rsqrt(jnp.mean(x * x, axis=-1, keepdims=True) + RMS_EPS)
    if final_norm:
        y_ref[0] = xn * gf_ref[...]
    hs = [(xn * g_ref[i:i + 1, :]).astype(BF16) for i in range(n_norm)]
    for seg in plan:
        w = w_refs[seg.w][:, seg.col * D_MODEL:(seg.col + 1) * D_MODEL]
        acc = jnp.dot(hs[seg.norm], w, preferred_element_type=F32)
        if seg.f32 is not None:
            f32_refs[seg.f32][0] = acc
        if seg.pm is not None:
            ab = (acc * seg.scale).astype(BF16) if seg.scale != 1.0 else acc.astype(BF16)
            for p in range(N_PAIRS):
                pm_refs[seg.pm][0, p] = ab[:, p * PAIR_W:(p + 1) * PAIR_W]


def _dense_call(x, *, og=None, wo=None, gains=None, weights=(), plan=(), gf=None,
                emit_x=False, f32_last_rows=None, tm=512, name="dense"):
    bx, tx, d = x.shape
    assert d == D_MODEL and tx % tm == 0
    has_resid = og is not None
    n_norm = 0 if gains is None else gains.shape[0]
    n_pm = sum(s.pm is not None for s in plan)
    n_f32 = sum(s.f32 is not None for s in plan)
    final_norm = gf is not None

    row_spec = pl.BlockSpec((1, tm, d), lambda b, t: (b, t, 0))
    pm_spec = pl.BlockSpec((1, N_PAIRS, tm, PAIR_W), lambda b, t: (b, 0, t, 0))
    whole = lambda a: pl.BlockSpec(a.shape, lambda b, t: (0,) * a.ndim)

    in_arrays, in_specs = [x], [row_spec]
    if has_resid:
        in_arrays += [og, wo]
        in_specs += [pm_spec, whole(wo)]
    if n_norm:
        in_arrays.append(gains)
        in_specs.append(whole(gains))
    for w in weights:
        in_arrays.append(w)
        in_specs.append(whole(w))
    if final_norm:
        in_arrays.append(gf)
        in_specs.append(whole(gf))

    out_shapes, out_specs = [], []
    if emit_x:
        out_shapes.append(jax.ShapeDtypeStruct((bx, tx, d), F32))
        out_specs.append(row_spec)
    for _ in range(n_pm):
        out_shapes.append(jax.ShapeDtypeStruct((bx, N_PAIRS, tx, PAIR_W), BF16))
        out_specs.append(pm_spec)
    for _ in range(n_f32):
        if f32_last_rows is None:
            out_shapes.append(jax.ShapeDtypeStruct((bx, tx, d), F32))
            out_specs.append(row_spec)
        else:
            assert f32_last_rows == tm
            out_shapes.append(jax.ShapeDtypeStruct((bx, tm, d), F32))
            out_specs.append(pl.BlockSpec((1, tm, d), lambda b, t: (b, 0, 0)))
    if final_norm:
        out_shapes.append(jax.ShapeDtypeStruct((bx, tx, d), F32))
        out_specs.append(row_spec)

    body = functools.partial(
        _dense_kernel, has_resid=has_resid, emit_x=emit_x, n_norm=n_norm, n_w=len(weights),
        plan=tuple(plan), final_norm=final_norm, n_pm=n_pm, n_f32=n_f32)
    return pl.pallas_call(
        body, grid=(bx, tx // tm), in_specs=in_specs, out_specs=out_specs, out_shape=out_shapes,
        name=name, compiler_params=_params(("arbitrary", "arbitrary")))(*in_arrays)


def _bias_kernel(rb_ref, bp_ref, bs_ref, *, past_len, ts):
    width = 768
    d0 = A_CACHE_ROWS
    rb = rb_ref[0]
    hi = rb.astype(BF16)
    r1 = rb - hi.astype(F32)
    mid = r1.astype(BF16)
    lo = (r1 - mid.astype(F32)).astype(BF16)
    c = lax.broadcasted_iota(jnp.int32, (N_REL_PAD, width), 0)
    n = lax.broadcasted_iota(jnp.int32, (N_REL_PAD, width), 1)
    m = jnp.where(n < BAND_TK, n, n - width)
    tgt = jnp.clip(d0 - m, -REL_CLIP, REL_CLIP) + REL_CLIP
    onehot = jnp.where(c == tgt, 1.0, 0.0).astype(BF16)
    r_ext = (jnp.dot(hi, onehot, preferred_element_type=F32)
             + jnp.dot(mid, onehot, preferred_element_type=F32)
             + jnp.dot(lo, onehot, preferred_element_type=F32))

    sub = lax.broadcasted_iota(jnp.int32, (8, width), 0)
    nks = A_CACHE_ROWS + ts
    for h2 in range(2):
        base = jnp.broadcast_to(r_ext[h2:h2 + 1, :], (8, width))
        b8 = base
        for r in range(1, 8):
            b8 = jnp.where(sub == r, pltpu.roll(base, r, 1), b8)
        blocks = [b8] + [pltpu.roll(b8, 8 * gi, 1) for gi in range(1, BAND_TQ // 8)]
        toep = jnp.concatenate(blocks, axis=0)[:, :BAND_TK]

        def masked(q0, k0, rows, cols):
            qpos = q0 + lax.broadcasted_iota(jnp.int32, (rows, cols), 0)
            kpos = k0 + lax.broadcasted_iota(jnp.int32, (rows, cols), 1)
            qc0 = qpos - jnp.bitwise_and(qpos, CHUNK - 1)
            lo_k = jnp.maximum(qc0 - A_CACHE_ROWS, 0)
            ok = jnp.logical_and(kpos >= lo_k, kpos < qc0 + CHUNK)
            return jnp.where(ok, toep[:rows, :cols], NEG_INF)

        for v in range(BAND_VARIANTS):
            q0 = BAND_TQ * v
            bp_ref[0, v, h2 * BAND_TQ:(h2 + 1) * BAND_TQ, :] = masked(q0, q0 - A_CACHE_ROWS, BAND_TQ, BAND_TK)
        bs_ref[0, h2 * ts:(h2 + 1) * ts, :] = masked(past_len, past_len - A_CACHE_ROWS, ts, nks)


def _bias_call(rel_bias, past_len, ts):
    rb = jnp.pad(rel_bias.reshape(N_PAIRS, 2, N_REL), ((0, 0), (0, 6), (0, N_REL_PAD - N_REL)))
    nks = A_CACHE_ROWS + ts
    return pl.pallas_call(
        functools.partial(_bias_kernel, past_len=past_len, ts=ts),
        grid=(N_PAIRS,),
        in_specs=[pl.BlockSpec((1, 8, N_REL_PAD), lambda p: (p, 0, 0))],
        out_specs=[pl.BlockSpec((1, BAND_VARIANTS, 2 * BAND_TQ, BAND_TK), lambda p: (p, 0, 0, 0)),
                   pl.BlockSpec((1, 2 * ts, nks), lambda p: (p, 0, 0))],
        out_shape=[jax.ShapeDtypeStruct((N_PAIRS, BAND_VARIANTS, 2 * BAND_TQ, BAND_TK), F32),
                   jax.ShapeDtypeStruct((N_PAIRS, 2 * ts, nks), F32)],
        name="band_bias", compiler_params=_params(("arbitrary",)))(rb)


def _first_head_lanes(rows):
    return lax.broadcasted_iota(jnp.int32, (rows, PAIR_W), 1) < HEAD_DIM


def _stack_heads(q, first):
    zero = jnp.zeros_like(q)
    return jnp.concatenate([jnp.where(first, q, zero), jnp.where(first, zero, q)], axis=0)


def _unstack_heads(o2, first):
    rows = o2.shape[0] // 2
    return jnp.where(first, o2[:rows], o2[rows:])


def _nt_dot(a, b):
    return lax.dot_general(a, b, (((1,), (1,)), ((), ())), preferred_element_type=F32)


def _gate(o, g):
    g = g.astype(F32)
    return (o * (g * jax.nn.sigmoid(g))).astype(BF16)


def _softmax_chains(q2s, ks, vs, biases):
    n = len(q2s)
    st = [dict() for _ in range(n)]

    def stage(c, s):
        d = st[c]
        if s == 0:
            d["s"] = _nt_dot(q2s[c], ks[c]) + biases[c]
        elif s == 1:
            d["m"] = jnp.max(d["s"], axis=-1, keepdims=True)
        elif s == 2:
            p = jnp.exp(d.pop("s") - d.pop("m"))
            d["l"] = jnp.sum(p, axis=-1, keepdims=True)
            d["p"] = p.astype(BF16)
        elif s == 3:
            d["o"] = jnp.dot(d.pop("p"), vs[c], preferred_element_type=F32)
        else:
            d["o"] = d["o"] * (1.0 / d.pop("l"))

    n_stages = 5
    for slot in range(n_stages + n - 1):
        for c in range(n):
            if 0 <= slot - c < n_stages:
                stage(c, slot - c)
    return [d["o"] for d in st]


def _band_prompt_kernel(q_ref, k_ref, v_ref, g_ref, bias_ref, o_ref, kpad, vpad, *, t):
    zeros = jnp.zeros((A_CACHE_ROWS, PAIR_W), BF16)
    kpad[0:A_CACHE_ROWS, :] = zeros
    vpad[0:A_CACHE_ROWS, :] = zeros
    kpad[A_CACHE_ROWS:, :] = k_ref[0, 0]
    vpad[A_CACHE_ROWS:, :] = v_ref[0, 0]
    first = _first_head_lanes(BAND_TQ)

    def body(i, carry):
        qts = [i * BAND_CHAINS + c for c in range(BAND_CHAINS)]
        r0s = [pl.multiple_of(qt * BAND_TQ, BAND_TQ) for qt in qts]
        outs = _softmax_chains(
            [_stack_heads(q_ref[0, 0, pl.ds(r0, BAND_TQ), :], first) for r0 in r0s],
            [kpad[pl.ds(r0, BAND_TK), :] for r0 in r0s],
            [vpad[pl.ds(r0, BAND_TK), :] for r0 in r0s],
            [bias_ref[0, jnp.minimum(qt, BAND_VARIANTS - 1)] for qt in qts])
        for r0, o2 in zip(r0s, outs):
            o = _unstack_heads(o2, first)
            o_ref[0, 0, pl.ds(r0, BAND_TQ), :] = _gate(o, g_ref[0, 0, pl.ds(r0, BAND_TQ), :])
        return carry

    assert (t // BAND_TQ) % BAND_CHAINS == 0
    lax.fori_loop(0, t // BAND_TQ // BAND_CHAINS, body, 0)


def _band_prompt_call(q, k, v, g, bias_p):
    b, _, t, _ = q.shape
    blk = pl.BlockSpec((1, 1, t, PAIR_W), lambda p, bb: (bb, p, 0, 0))
    return pl.pallas_call(
        functools.partial(_band_prompt_kernel, t=t),
        grid=(N_PAIRS, b),
        in_specs=[blk, blk, blk, blk,
                  pl.BlockSpec((1, BAND_VARIANTS, 2 * BAND_TQ, BAND_TK), lambda p, bb: (p, 0, 0, 0))],
        out_specs=blk,
        out_shape=jax.ShapeDtypeStruct(q.shape, BF16),
        scratch_shapes=[pltpu.VMEM((t + A_CACHE_ROWS, PAIR_W), BF16),
                        pltpu.VMEM((t + A_CACHE_ROWS, PAIR_W), BF16)],
        name="band_prompt", compiler_params=_params(("arbitrary", "arbitrary")))(q, k, v, g, bias_p)


def _band_sample_kernel(q_ref, k_ref, v_ref, g_ref, ck_ref, cv_ref, bias_ref, o_ref, *, ts):
    first = _first_head_lanes(ts)
    pairs = range(N_PAIRS)
    lanes = [slice(p * PAIR_W, (p + 1) * PAIR_W) for p in pairs]
    outs = _softmax_chains(
        [_stack_heads(q_ref[0, p], first) for p in pairs],
        [jnp.concatenate([ck_ref[0, :, lanes[p]].astype(BF16), k_ref[0, p]], axis=0) for p in pairs],
        [jnp.concatenate([cv_ref[0, :, lanes[p]].astype(BF16), v_ref[0, p]], axis=0) for p in pairs],
        [bias_ref[p] for p in pairs])
    for p in pairs:
        o_ref[0, p] = _gate(_unstack_heads(outs[p], first), g_ref[0, p])


def _band_sample_call(q, k, v, g, cache_k, cache_v, bias_s, bs, ts):
    blk = pl.BlockSpec((1, N_PAIRS, ts, PAIR_W), lambda b: (0, 0, b, 0))
    cblk = pl.BlockSpec((1, A_CACHE_ROWS, D_MODEL), lambda b: (b, 0, 0))
    return pl.pallas_call(
        functools.partial(_band_sample_kernel, ts=ts),
        grid=(bs,),
        in_specs=[blk, blk, blk, blk, cblk, cblk,
                  pl.BlockSpec(bias_s.shape, lambda b: (0, 0, 0))],
        out_specs=blk,
        out_shape=jax.ShapeDtypeStruct(q.shape, BF16),
        name="band_sample", compiler_params=_params(("arbitrary",)))(q, k, v, g, cache_k, cache_v, bias_s)


def _suffix_matrix(n):
    r = lax.broadcasted_iota(jnp.int32, (n, n), 0)
    c = lax.broadcasted_iota(jnp.int32, (n, n), 1)
    return jnp.where(r > c, 1.0, 0.0).astype(BF16)


def _sb_tiles(q2s, kts, vts, suffix, carries, accs, causal):
    n = len(q2s)
    st = [dict() for _ in range(n)]

    def stage(c, s):
        d = st[c]
        if s == 0:
            d["z"] = _nt_dot(q2s[c], kts[c])
        elif s == 1:
            z_all = d.pop("z")
            hls, zss, sums = [], [], []
            for r0 in range(0, z_all.shape[0], SB_ROW_BLOCK):
                rs = slice(r0, r0 + SB_ROW_BLOCK)
                z = z_all[rs]
                neg_abs = pltpu.bitcast(pltpu.bitcast(z, jnp.uint32) | jnp.uint32(0x80000000), F32)
                sp = jnp.maximum(z, 0.0) + jnp.log(1.0 + jnp.exp2(neg_abs)) * LOG2E
                if causal is not None:
                    sp = jnp.where(causal[rs], sp, 0.0)
                hls.append(sp.astype(BF16))
                zss.append((z - sp) - carries[c][rs])
                sums.append(jnp.sum(sp, axis=-1, keepdims=True))
            d["hl"] = jnp.concatenate(hls, axis=0)
            d["zs"] = zss
            d["sum"] = jnp.concatenate(sums, axis=0)
        elif s == 2:
            d["later"] = jnp.dot(d.pop("hl"), suffix, preferred_element_type=F32)
        elif s == 3:
            later_all = d.pop("later")
            ws = []
            for i, zs in enumerate(d.pop("zs")):
                rs = slice(i * SB_ROW_BLOCK, (i + 1) * SB_ROW_BLOCK)
                w = jnp.exp2(zs - later_all[rs])
                if causal is not None:
                    w = jnp.where(causal[rs], w, 0.0)
                ws.append(w.astype(BF16))
            d["w"] = jnp.concatenate(ws, axis=0)
        else:
            d["acc"] = accs[c] + jnp.dot(d.pop("w"), vts[c], preferred_element_type=F32)

    n_stages = 5
    for slot in range(n_stages + n - 1):
        for c in range(n):
            if 0 <= slot - c < n_stages:
                stage(c, slot - c)
    out = []
    for c in range(n):
        out += [carries[c] + st[c]["sum"], st[c]["acc"]]
    return out


def _sb_prompt_kernel(q_ref, k_ref, v_ref, g_ref, o_ref, *, t):
    first = _first_head_lanes(SB_TQ)
    suffix = _suffix_matrix(SB_TK)
    row = lax.broadcasted_iota(jnp.int32, (SB_TQ, SB_TK), 0)
    col = lax.broadcasted_iota(jnp.int32, (SB_TQ, SB_TK), 1)
    causal = col < row

    pairs = range(SB_CHAINS)
    chain_pair = [p for p in pairs for _ in range(2)]

    def q_body(qt, c0):
        r0 = pl.multiple_of(qt * SB_TQ, SB_TQ)
        qh = []
        for p in pairs:
            q = q_ref[0, p, pl.ds(r0, SB_TQ), :]
            zero = jnp.zeros_like(q)
            qh += [jnp.where(first, q, zero), jnp.where(first, zero, q)]

        def tiles(k0, carries, accs, mask):
            return _sb_tiles(qh, [k_ref[0, p, pl.ds(k0, SB_TK), :] for p in chain_pair],
                             [v_ref[0, p, pl.ds(k0, SB_TK), :] for p in chain_pair], suffix,
                             carries, accs, mask)

        state = tiles(r0, [jnp.zeros((SB_TQ, 1), F32) for _ in chain_pair],
                      [jnp.zeros((SB_TQ, PAIR_W), F32) for _ in chain_pair], causal)

        def k_body(i, st):
            k0 = pl.multiple_of((qt - 1 - i) * SB_TK, SB_TK)
            return tuple(tiles(k0, st[0::2], st[1::2], None))

        state = lax.fori_loop(0, qt, k_body, tuple(state))
        for p in pairs:
            o = jnp.where(first, state[4 * p + 1], state[4 * p + 3])
            o_ref[0, p, pl.ds(r0, SB_TQ), :] = _gate(o, g_ref[0, p, pl.ds(r0, SB_TQ), :])
        return c0

    lax.fori_loop(0, t // SB_TQ, q_body, 0)


def _sb_prompt_call(q, k, v, g):
    b, _, t, _ = q.shape
    blk = pl.BlockSpec((1, SB_CHAINS, t, PAIR_W), lambda bb, p: (bb, p, 0, 0))
    return pl.pallas_call(
        functools.partial(_sb_prompt_kernel, t=t),
        grid=(b, N_PAIRS // SB_CHAINS),
        in_specs=[blk, blk, blk, blk],
        out_specs=blk,
        out_shape=jax.ShapeDtypeStruct(q.shape, BF16),
        name="sb_prompt", compiler_params=_params(("arbitrary", "arbitrary")))(q, k, v, g)


def _sb_sample_kernel(q_ref, k_ref, v_ref, g_ref, ck_ref, cv_ref, o_ref, carry_ref, acc_ref, *, ts, n_kt):
    step = pl.program_id(1)
    first = _first_head_lanes(ts)

    @pl.when(step == 0)
    def _():
        suffix = _suffix_matrix(ts)
        row = lax.broadcasted_iota(jnp.int32, (2 * ts, ts), 0)
        col = lax.broadcasted_iota(jnp.int32, (2 * ts, ts), 1)
        causal = col < jnp.bitwise_and(row, ts - 1)
        pairs = range(N_PAIRS)
        out = _sb_tiles([_stack_heads(q_ref[0, p], first) for p in pairs],
                        [k_ref[0, p] for p in pairs], [v_ref[0, p] for p in pairs], suffix,
                        [jnp.zeros((2 * ts, 1), F32) for _ in pairs],
                        [jnp.zeros((2 * ts, PAIR_W), F32) for _ in pairs], causal)
        for p in pairs:
            carry_ref[p] = out[2 * p]
            acc_ref[p] = out[2 * p + 1]

    @pl.when(step > 0)
    def _():
        suffix = _suffix_matrix(SB_TK)
        pairs = range(N_PAIRS)
        lanes = [slice(p * PAIR_W, (p + 1) * PAIR_W) for p in pairs]
        out = _sb_tiles([_stack_heads(q_ref[0, p], first) for p in pairs],
                        [ck_ref[0, :, lanes[p]].astype(BF16) for p in pairs],
                        [cv_ref[0, :, lanes[p]].astype(BF16) for p in pairs], suffix,
                        [carry_ref[p] for p in pairs], [acc_ref[p] for p in pairs], None)
        for p in pairs:
            carry_ref[p] = out[2 * p]
            acc_ref[p] = out[2 * p + 1]

    @pl.when(step == n_kt)
    def _():
        for p in range(N_PAIRS):
            o_ref[0, p] = _gate(_unstack_heads(acc_ref[p], first), g_ref[0, p])


def _sb_sample_call(q, k, v, g, cache_k, cache_v, bs, ts):
    past = cache_k.shape[1]
    assert past % SB_TK == 0
    n_kt = past // SB_TK
    blk = pl.BlockSpec((1, N_PAIRS, ts, PAIR_W), lambda b, s: (0, 0, b, 0))
    cblk = pl.BlockSpec((1, SB_TK, D_MODEL), lambda b, s: (b, n_kt - jnp.maximum(s, 1), 0))
    return pl.pallas_call(
        functools.partial(_sb_sample_kernel, ts=ts, n_kt=n_kt),
        grid=(bs, n_kt + 1),
        in_specs=[blk, blk, blk, blk, cblk, cblk],
        out_specs=blk,
        out_shape=jax.ShapeDtypeStruct(q.shape, BF16),
        scratch_shapes=[pltpu.VMEM((N_PAIRS, 2 * ts, 1), F32),
                        pltpu.VMEM((N_PAIRS, 2 * ts, PAIR_W), F32)],
        name="sb_sample", compiler_params=_params(("arbitrary", "arbitrary")))(q, k, v, g, cache_k, cache_v)


def kernel(x_prompt, x_sample, cache_a_k, cache_a_v, cache_b_k, cache_b_v, norm_a, w_in_a, rel_bias_a,
           w_out_a, norm_kv, w_kv, norm_b, w_in_b, w_out_b, norm_f):
    b, t, d = x_prompt.shape
    bs, ts, _ = x_sample.shape
    past = cache_b_k.shape[1]
    assert d == D_MODEL and norm_a.shape[0] == 1 and norm_b.shape[0] == 1
    assert cache_a_k.shape[2] == A_CACHE_ROWS and t % SB_TQ == 0 and t >= A_CACHE_ROWS
    assert past % CHUNK == 0 and ts == CHUNK

    w_a = w_in_a[0].astype(BF16)
    wo_a = w_out_a[0].astype(BF16)
    w_kvb = w_kv.astype(BF16)
    w_b = w_in_b[0].astype(BF16)
    wo_b = w_out_b[0].astype(BF16)
    g_a = norm_a
    g_b = jnp.stack([norm_kv, norm_b[0]])
    g_f = norm_f[None]
    xs = x_sample.reshape(1, bs * ts, d)

    bias_p, bias_s = _bias_call(rel_bias_a[0], past, ts)

    plan_a = (Seg(0, 0, 0, SCALE, 0, None), Seg(0, 0, 1, 1.0, 1, 0),
              Seg(0, 0, 2, 1.0, 2, 1), Seg(0, 0, 3, 1.0, 3, None))
    plan_b = (Seg(0, 0, 0, 1.0, 0, 0), Seg(0, 0, 1, 1.0, 1, 1),
              Seg(1, 1, 0, SCALE * LOG2E, 2, None), Seg(1, 1, 1, 1.0, 3, None))

    qp, kp, vp, gp, akp, avp = _dense_call(x_prompt, gains=g_a, weights=(w_a,), plan=plan_a,
                                           f32_last_rows=A_CACHE_ROWS, name="proj_a_prompt")
    qs, ks, vs, gs, aks, avs = _dense_call(xs, gains=g_a, weights=(w_a,), plan=plan_a,
                                           name="proj_a_sample")

    ogp = _band_prompt_call(qp, kp, vp, gp, bias_p)
    ogs = _band_sample_call(qs, ks, vs, gs, cache_a_k[0].reshape(bs, A_CACHE_ROWS, d),
                            cache_a_v[0].reshape(bs, A_CACHE_ROWS, d), bias_s, bs, ts)

    xp1, kbp, vbp, qbp, gbp, kbp32, vbp32 = _dense_call(
        x_prompt, og=ogp, wo=wo_a, gains=g_b, weights=(w_kvb, w_b), plan=plan_b, emit_x=True, tm=256,
        name="out_a_proj_b_prompt")
    xs1, kbs, vbs, qbs, gbs, kbs32, vbs32 = _dense_call(
        xs, og=ogs, wo=wo_a, gains=g_b, weights=(w_kvb, w_b), plan=plan_b, emit_x=True, tm=256,
        name="out_a_proj_b_sample")

    obp = _sb_prompt_call(qbp, kbp, vbp, gbp)
    obs = _sb_sample_call(qbs, kbs, vbs, gbs, cache_b_k.reshape(bs, past, d),
                          cache_b_v.reshape(bs, past, d), bs, ts)

    (y_prompt,) = _dense_call(xp1, og=obp, wo=wo_b, gf=g_f, name="out_b_prompt")
    (y_sample,) = _dense_call(xs1, og=obs, wo=wo_b, gf=g_f, name="out_b_sample")

    heads = lambda a, n, rows: a.reshape(n, rows, N_HEADS, HEAD_DIM)
    return (y_prompt, y_sample.reshape(bs, ts, d),
            heads(akp, b, A_CACHE_ROWS)[None], heads(avp, b, A_CACHE_ROWS)[None],
            heads(kbp32, b, t), heads(vbp32, b, t),
            heads(aks, bs, ts)[None], heads(avs, bs, ts)[None],
            heads(kbs32, bs, ts), heads(vbs32, bs, ts))
```

```python
import functools
from typing import NamedTuple, Optional

import jax
import jax.numpy as jnp
from jax import lax
from jax.experimental import pallas as pl
from jax.experimental.pallas import tpu as pltpu

D_MODEL = 1024
N_HEADS = 16
HEAD_DIM = 64
PAIR_W = 2 * HEAD_DIM
N_PAIRS = N_HEADS // 2
CHUNK = 64
LEFT_CHUNKS = 8
A_CACHE_ROWS = LEFT_CHUNKS * CHUNK
REL_CLIP = 128
N_REL = 2 * REL_CLIP + 1
N_REL_PAD = 384
RMS_EPS = 1e-6
NEG_INF = -1e30
SCALE = HEAD_DIM ** -0.5
LOG2E = 1.4426950408889634

BAND_TQ = 128
BAND_TK = A_CACHE_ROWS + BAND_TQ
BAND_VARIANTS = A_CACHE_ROWS // BAND_TQ + 1
BAND_CHAINS = 8
SB_TQ = 256
SB_TK = 256
SB_CHAINS = 4
SB_Z_CLAMP = 126.0

VMEM_LIMIT = 56 * 1024 * 1024

F32 = jnp.float32
BF16 = jnp.bfloat16


def _params(sem):
    return pltpu.CompilerParams(dimension_semantics=sem, vmem_limit_bytes=VMEM_LIMIT)


class Seg(NamedTuple):
    norm: int
    w: int
    col: int
    scale: float
    pm: Optional[int]
    f32: Optional[int]


def _dense_kernel(*refs, has_resid, emit_x, n_norm, n_w, plan, final_norm, n_pm, n_f32):
    it = iter(refs)
    x_ref = next(it)
    og_ref = next(it) if has_resid else None
    wo_ref = next(it) if has_resid else None
    g_ref = next(it) if n_norm else None
    w_refs = [next(it) for _ in range(n_w)]
    gf_ref = next(it) if final_norm else None
    xo_ref = next(it) if emit_x else None
    pm_refs = [next(it) for _ in range(n_pm)]
    f32_refs = [next(it) for _ in range(n_f32)]
    y_ref = next(it) if final_norm else None

    x = x_ref[0]
    if has_resid:
        og = jnp.concatenate([og_ref[0, p] for p in range(N_PAIRS)], axis=1)
        x = x + jnp.dot(og, wo_ref[...], preferred_element_type=F32)
        if emit_x:
            xo_ref[0] = x
    if n_norm or final_norm:
        xn = x * lax.rsqrt(jnp.mean(x * x, axis=-1, keepdims=True) + RMS_EPS)
    if final_norm:
        y_ref[0] = xn * gf_ref[...]
    hs = [(xn * g_ref[i:i + 1, :]).astype(BF16) for i in range(n_norm)]
    for seg in plan:
        w = w_refs[seg.w][:, seg.col * D_MODEL:(seg.col + 1) * D_MODEL]
        acc = jnp.dot(hs[seg.norm], w, preferred_element_type=F32)
        if seg.f32 is not None:
            f32_refs[seg.f32][0] = acc
        if seg.pm is not None:
            ab = (acc * seg.scale).astype(BF16) if seg.scale != 1.0 else acc.astype(BF16)
            for p in range(N_PAIRS):
                pm_refs[seg.pm][0, p] = ab[:, p * PAIR_W:(p + 1) * PAIR_W]


def _dense_call(x, *, og=None, wo=None, gains=None, weights=(), plan=(), gf=None,
                emit_x=False, f32_last_rows=None, tm=512, name="dense"):
    bx, tx, d = x.shape
    assert d == D_MODEL and tx % tm == 0
    has_resid = og is not None
    n_norm = 0 if gains is None else gains.shape[0]
    n_pm = sum(s.pm is not None for s in plan)
    n_f32 = sum(s.f32 is not None for s in plan)
    final_norm = gf is not None

    row_spec = pl.BlockSpec((1, tm, d), lambda b, t: (b, t, 0))
    pm_spec = pl.BlockSpec((1, N_PAIRS, tm, PAIR_W), lambda b, t: (b, 0, t, 0))
    whole = lambda a: pl.BlockSpec(a.shape, lambda b, t: (0,) * a.ndim)

    in_arrays, in_specs = [x], [row_spec]
    if has_resid:
        in_arrays += [og, wo]
        in_specs += [pm_spec, whole(wo)]
    if n_norm:
        in_arrays.append(gains)
        in_specs.append(whole(gains))
    for w in weights:
        in_arrays.append(w)
        in_specs.append(whole(w))
    if final_norm:
        in_arrays.append(gf)
        in_specs.append(whole(gf))

    out_shapes, out_specs = [], []
    if emit_x:
        out_shapes.append(jax.ShapeDtypeStruct((bx, tx, d), F32))
        out_specs.append(row_spec)
    for _ in range(n_pm):
        out_shapes.append(jax.ShapeDtypeStruct((bx, N_PAIRS, tx, PAIR_W), BF16))
        out_specs.append(pm_spec)
    for _ in range(n_f32):
        if f32_last_rows is None:
            out_shapes.append(jax.ShapeDtypeStruct((bx, tx, d), F32))
            out_specs.append(row_spec)
        else:
            assert f32_last_rows == tm
            out_shapes.append(jax.ShapeDtypeStruct((bx, tm, d), F32))
            out_specs.append(pl.BlockSpec((1, tm, d), lambda b, t: (b, 0, 0)))
    if final_norm:
        out_shapes.append(jax.ShapeDtypeStruct((bx, tx, d), F32))
        out_specs.append(row_spec)

    body = functools.partial(
        _dense_kernel, has_resid=has_resid, emit_x=emit_x, n_norm=n_norm, n_w=len(weights),
        plan=tuple(plan), final_norm=final_norm, n_pm=n_pm, n_f32=n_f32)
    return pl.pallas_call(
        body, grid=(bx, tx // tm), in_specs=in_specs, out_specs=out_specs, out_shape=out_shapes,
        name=name, compiler_params=_params(("arbitrary", "arbitrary")))(*in_arrays)


def _bias_kernel(rb_ref, bp_ref, bs_ref, *, past_len, ts):
    width = 768
    d0 = A_CACHE_ROWS
    rb = rb_ref[0]
    hi = rb.astype(BF16)
    r1 = rb - hi.astype(F32)
    mid = r1.astype(BF16)
    lo = (r1 - mid.astype(F32)).astype(BF16)
    c = lax.broadcasted_iota(jnp.int32, (N_REL_PAD, width), 0)
    n = lax.broadcasted_iota(jnp.int32, (N_REL_PAD, width), 1)
    m = jnp.where(n < BAND_TK, n, n - width)
    tgt = jnp.clip(d0 - m, -REL_CLIP, REL_CLIP) + REL_CLIP
    onehot = jnp.where(c == tgt, 1.0, 0.0).astype(BF16)
    r_ext = (jnp.dot(hi, onehot, preferred_element_type=F32)
             + jnp.dot(mid, onehot, preferred_element_type=F32)
             + jnp.dot(lo, onehot, preferred_element_type=F32))

    sub = lax.broadcasted_iota(jnp.int32, (8, width), 0)
    nks = A_CACHE_ROWS + ts
    for h2 in range(2):
        base = jnp.broadcast_to(r_ext[h2:h2 + 1, :], (8, width))
        b8 = base
        for r in range(1, 8):
            b8 = jnp.where(sub == r, pltpu.roll(base, r, 1), b8)
        blocks = [b8] + [pltpu.roll(b8, 8 * gi, 1) for gi in range(1, BAND_TQ // 8)]
        toep = jnp.concatenate(blocks, axis=0)[:, :BAND_TK]

        def masked(q0, k0, rows, cols):
            qpos = q0 + lax.broadcasted_iota(jnp.int32, (rows, cols), 0)
            kpos = k0 + lax.broadcasted_iota(jnp.int32, (rows, cols), 1)
            qc0 = qpos - jnp.bitwise_and(qpos, CHUNK - 1)
            lo_k = jnp.maximum(qc0 - A_CACHE_ROWS, 0)
            ok = jnp.logical_and(kpos >= lo_k, kpos < qc0 + CHUNK)
            return jnp.where(ok, toep[:rows, :cols] * LOG2E, NEG_INF)

        for v in range(BAND_VARIANTS):
            q0 = BAND_TQ * v
            bp_ref[0, v, h2 * BAND_TQ:(h2 + 1) * BAND_TQ, :] = masked(q0, q0 - A_CACHE_ROWS, BAND_TQ, BAND_TK)
        bs_ref[0, h2 * ts:(h2 + 1) * ts, :] = masked(past_len, past_len - A_CACHE_ROWS, ts, nks)


def _bias_call(rel_bias, past_len, ts):
    rb = jnp.pad(rel_bias.reshape(N_PAIRS, 2, N_REL), ((0, 0), (0, 6), (0, N_REL_PAD - N_REL)))
    nks = A_CACHE_ROWS + ts
    return pl.pallas_call(
        functools.partial(_bias_kernel, past_len=past_len, ts=ts),
        grid=(N_PAIRS,),
        in_specs=[pl.BlockSpec((1, 8, N_REL_PAD), lambda p: (p, 0, 0))],
        out_specs=[pl.BlockSpec((1, BAND_VARIANTS, 2 * BAND_TQ, BAND_TK), lambda p: (p, 0, 0, 0)),
                   pl.BlockSpec((1, 2 * ts, nks), lambda p: (p, 0, 0))],
        out_shape=[jax.ShapeDtypeStruct((N_PAIRS, BAND_VARIANTS, 2 * BAND_TQ, BAND_TK), F32),
                   jax.ShapeDtypeStruct((N_PAIRS, 2 * ts, nks), F32)],
        name="band_bias", compiler_params=_params(("arbitrary",)))(rb)


def _first_head_lanes(rows):
    return lax.broadcasted_iota(jnp.int32, (rows, PAIR_W), 1) < HEAD_DIM


def _stack_heads(q, first):
    zero = jnp.zeros_like(q)
    return jnp.concatenate([jnp.where(first, q, zero), jnp.where(first, zero, q)], axis=0)


def _unstack_heads(o2, first):
    rows = o2.shape[0] // 2
    return jnp.where(first, o2[:rows], o2[rows:])


def _nt_dot(a, b):
    return lax.dot_general(a, b, (((1,), (1,)), ((), ())), preferred_element_type=F32)


def _gate(o, g):
    g = g.astype(F32)
    return (o * (g * jax.nn.sigmoid(g))).astype(BF16)


def _nn_dot(a, b):
    return jnp.dot(a, b, preferred_element_type=F32)


def _skewed(n_chains, n_stages, stage):
    for slot in range(n_stages + n_chains - 1):
        for c in range(n_chains):
            if 0 <= slot - c < n_stages:
                stage(c, slot - c)


def _softmax_chains(score_fns, pv_fns):
    n = len(score_fns)
    st = [dict() for _ in range(n)]

    def stage(c, s):
        d = st[c]
        if s == 0:
            d["s"] = score_fns[c]()
        elif s == 1:
            d["m"] = jnp.max(d["s"], axis=-1, keepdims=True)
        elif s == 2:
            p = jnp.exp2(d.pop("s") - d.pop("m"))
            d["l"] = jnp.sum(p, axis=-1, keepdims=True)
            d["p"] = p.astype(BF16)
        elif s == 3:
            d["o"] = pv_fns[c](d.pop("p"))
        else:
            d["o"] = d["o"] * (1.0 / d.pop("l"))

    _skewed(n, 5, stage)
    return [d["o"] for d in st]


def _band_prompt_kernel(q_ref, k_ref, v_ref, g_ref, bias_ref, o_ref, kpad, vpad, *, t):
    zeros = jnp.zeros((A_CACHE_ROWS, PAIR_W), BF16)
    kpad[0:A_CACHE_ROWS, :] = zeros
    vpad[0:A_CACHE_ROWS, :] = zeros
    kpad[A_CACHE_ROWS:, :] = k_ref[0, 0]
    vpad[A_CACHE_ROWS:, :] = v_ref[0, 0]
    first = _first_head_lanes(BAND_TQ)

    def body(i, carry):
        qts = [i * BAND_CHAINS + c for c in range(BAND_CHAINS)]
        r0s = [pl.multiple_of(qt * BAND_TQ, BAND_TQ) for qt in qts]
        def score_fn(qt, r0):
            def fn():
                q2 = _stack_heads(q_ref[0, 0, pl.ds(r0, BAND_TQ), :], first)
                kw = kpad[pl.ds(r0, BAND_TK), :]
                return _nt_dot(q2, kw) + bias_ref[0, jnp.minimum(qt, BAND_VARIANTS - 1)]
            return fn

        outs = _softmax_chains(
            [score_fn(qt, r0) for qt, r0 in zip(qts, r0s)],
            [lambda p, r0=r0: _nn_dot(p, vpad[pl.ds(r0, BAND_TK), :]) for r0 in r0s])
        for r0, o2 in zip(r0s, outs):
            o = _unstack_heads(o2, first)
            o_ref[0, 0, pl.ds(r0, BAND_TQ), :] = _gate(o, g_ref[0, 0, pl.ds(r0, BAND_TQ), :])
        return carry

    assert (t // BAND_TQ) % BAND_CHAINS == 0
    lax.fori_loop(0, t // BAND_TQ // BAND_CHAINS, body, 0)


def _band_prompt_call(q, k, v, g, bias_p):
    b, _, t, _ = q.shape
    blk = pl.BlockSpec((1, 1, t, PAIR_W), lambda p, bb: (bb, p, 0, 0))
    return pl.pallas_call(
        functools.partial(_band_prompt_kernel, t=t),
        grid=(N_PAIRS, b),
        in_specs=[blk, blk, blk, blk,
                  pl.BlockSpec((1, BAND_VARIANTS, 2 * BAND_TQ, BAND_TK), lambda p, bb: (p, 0, 0, 0))],
        out_specs=blk,
        out_shape=jax.ShapeDtypeStruct(q.shape, BF16),
        scratch_shapes=[pltpu.VMEM((t + A_CACHE_ROWS, PAIR_W), BF16),
                        pltpu.VMEM((t + A_CACHE_ROWS, PAIR_W), BF16)],
        name="band_prompt", compiler_params=_params(("arbitrary", "arbitrary")))(q, k, v, g, bias_p)


def _band_sample_kernel(q_ref, k_ref, v_ref, g_ref, ckt_ref, cvt_ref, bias_ref, o_ref, *, ts):
    first = _first_head_lanes(ts)
    pairs = range(N_PAIRS)
    rows = [slice(p * PAIR_W, (p + 1) * PAIR_W) for p in pairs]

    def score_fn(p):
        def fn():
            q2 = _stack_heads(q_ref[0, p], first)
            s_old = _nn_dot(q2, ckt_ref[0, rows[p], :].astype(BF16))
            s_new = _nt_dot(q2, k_ref[0, p])
            return jnp.concatenate([s_old, s_new], axis=1) + bias_ref[p]
        return fn

    def pv_fn(p):
        def fn(prob):
            return (_nt_dot(prob[:, :A_CACHE_ROWS], cvt_ref[0, rows[p], :].astype(BF16))
                    + _nn_dot(prob[:, A_CACHE_ROWS:], v_ref[0, p]))
        return fn

    outs = _softmax_chains([score_fn(p) for p in pairs], [pv_fn(p) for p in pairs])
    for p in pairs:
        o_ref[0, p] = _gate(_unstack_heads(outs[p], first), g_ref[0, p])


def _band_sample_call(q, k, v, g, cache_kt, cache_vt, bias_s, bs, ts):
    blk = pl.BlockSpec((1, N_PAIRS, ts, PAIR_W), lambda b: (0, 0, b, 0))
    cblk = pl.BlockSpec((1, D_MODEL, A_CACHE_ROWS), lambda b: (b, 0, 0))
    return pl.pallas_call(
        functools.partial(_band_sample_kernel, ts=ts),
        grid=(bs,),
        in_specs=[blk, blk, blk, blk, cblk, cblk,
                  pl.BlockSpec(bias_s.shape, lambda b: (0, 0, 0))],
        out_specs=blk,
        out_shape=jax.ShapeDtypeStruct(q.shape, BF16),
        name="band_sample", compiler_params=_params(("arbitrary",)))(q, k, v, g, cache_kt, cache_vt, bias_s)


def _suffix_matrix(n):
    r = lax.broadcasted_iota(jnp.int32, (n, n), 0)
    c = lax.broadcasted_iota(jnp.int32, (n, n), 1)
    return jnp.where(r > c, 1.0, 0.0).astype(BF16)


def _sb_tiles(score_fns, pv_fns, suffix, carries, accs, causal):
    n = len(score_fns)
    st = [dict() for _ in range(n)]

    def stage(c, s):
        d = st[c]
        if s == 0:
            d["z"] = score_fns[c]()
        elif s == 1:
            z = d.pop("z")
            sp = jnp.maximum(jnp.log(1.0 + jnp.exp2(jnp.minimum(z, SB_Z_CLAMP))) * LOG2E, z)
            if causal is not None:
                sp = jnp.where(causal, sp, 0.0)
            d["sp"] = sp.astype(BF16)
            d["zs"] = (z - sp) - carries[c]
            d["sum"] = jnp.sum(sp, axis=-1, keepdims=True)
        elif s == 2:
            d["later"] = _nn_dot(d.pop("sp"), suffix)
        elif s == 3:
            w = jnp.exp2(d.pop("zs") - d.pop("later"))
            if causal is not None:
                w = jnp.where(causal, w, 0.0)
            d["w"] = w.astype(BF16)
        else:
            d["acc"] = accs[c] + pv_fns[c](d.pop("w"))

    _skewed(n, 5, stage)
    out = []
    for c in range(n):
        out += [carries[c] + st[c]["sum"], st[c]["acc"]]
    return out


def _sb_prompt_kernel(q_ref, k_ref, v_ref, g_ref, o_ref, *, t):
    first = _first_head_lanes(SB_TQ)
    suffix = _suffix_matrix(SB_TK)
    row = lax.broadcasted_iota(jnp.int32, (SB_TQ, SB_TK), 0)
    col = lax.broadcasted_iota(jnp.int32, (SB_TQ, SB_TK), 1)
    causal = col < row

    pairs = range(SB_CHAINS)
    chain_pair = [p for p in pairs for _ in range(2)]

    def q_body(qt, c0):
        r0 = pl.multiple_of(qt * SB_TQ, SB_TQ)
        qh = []
        for p in pairs:
            q = q_ref[0, p, pl.ds(r0, SB_TQ), :]
            zero = jnp.zeros_like(q)
            qh += [jnp.where(first, q, zero), jnp.where(first, zero, q)]

        def tiles(k0, carries, accs, mask):
            return _sb_tiles(
                [lambda c=c, p=p: _nt_dot(qh[c], k_ref[0, p, pl.ds(k0, SB_TK), :])
                 for c, p in enumerate(chain_pair)],
                [lambda w, p=p: _nn_dot(w, v_ref[0, p, pl.ds(k0, SB_TK), :]) for p in chain_pair],
                suffix, carries, accs, mask)

        state = tiles(r0, [jnp.zeros((SB_TQ, 1), F32) for _ in chain_pair],
                      [jnp.zeros((SB_TQ, PAIR_W), F32) for _ in chain_pair], causal)

        def k_body(i, st):
            k0 = pl.multiple_of((qt - 1 - i) * SB_TK, SB_TK)
            return tuple(tiles(k0, st[0::2], st[1::2], None))

        state = lax.fori_loop(0, qt, k_body, tuple(state))
        for p in pairs:
            o = jnp.where(first, state[4 * p + 1], state[4 * p + 3])
            o_ref[0, p, pl.ds(r0, SB_TQ), :] = _gate(o, g_ref[0, p, pl.ds(r0, SB_TQ), :])
        return c0

    lax.fori_loop(0, t // SB_TQ, q_body, 0)


def _sb_prompt_call(q, k, v, g):
    b, _, t, _ = q.shape
    blk = pl.BlockSpec((1, SB_CHAINS, t, PAIR_W), lambda bb, p: (bb, p, 0, 0))
    return pl.pallas_call(
        functools.partial(_sb_prompt_kernel, t=t),
        grid=(b, N_PAIRS // SB_CHAINS),
        in_specs=[blk, blk, blk, blk],
        out_specs=blk,
        out_shape=jax.ShapeDtypeStruct(q.shape, BF16),
        name="sb_prompt", compiler_params=_params(("arbitrary", "arbitrary")))(q, k, v, g)


def _sb_sample_kernel(q_ref, k_ref, v_ref, g_ref, ckt_ref, cvt_ref, o_ref, carry_ref, acc_ref, *, ts, n_kt):
    step = pl.program_id(1)
    first = _first_head_lanes(ts)
    pairs = range(N_PAIRS)
    rows = [slice(p * PAIR_W, (p + 1) * PAIR_W) for p in pairs]

    def save(out):
        for p in pairs:
            carry_ref[p] = out[2 * p]
            acc_ref[p] = out[2 * p + 1]

    @pl.when(step == 0)
    def _():
        row = lax.broadcasted_iota(jnp.int32, (2 * ts, ts), 0)
        col = lax.broadcasted_iota(jnp.int32, (2 * ts, ts), 1)
        causal = col < jnp.bitwise_and(row, ts - 1)
        save(_sb_tiles([lambda p=p: _nt_dot(_stack_heads(q_ref[0, p], first), k_ref[0, p]) for p in pairs],
                       [lambda w, p=p: _nn_dot(w, v_ref[0, p]) for p in pairs], _suffix_matrix(ts),
                       [jnp.zeros((2 * ts, 1), F32) for _ in pairs],
                       [jnp.zeros((2 * ts, PAIR_W), F32) for _ in pairs], causal))

    @pl.when(step > 0)
    def _():
        save(_sb_tiles([lambda p=p: _nn_dot(_stack_heads(q_ref[0, p], first), ckt_ref[0, rows[p], :].astype(BF16))
                        for p in pairs],
                       [lambda w, p=p: _nt_dot(w, cvt_ref[0, rows[p], :].astype(BF16)) for p in pairs],
                       _suffix_matrix(SB_TK), [carry_ref[p] for p in pairs], [acc_ref[p] for p in pairs], None))

    @pl.when(step == n_kt)
    def _():
        for p in range(N_PAIRS):
            o_ref[0, p] = _gate(_unstack_heads(acc_ref[p], first), g_ref[0, p])


def _sb_sample_call(q, k, v, g, cache_kt, cache_vt, bs, ts):
    past = cache_kt.shape[2]
    assert past % SB_TK == 0
    n_kt = past // SB_TK
    blk = pl.BlockSpec((1, N_PAIRS, ts, PAIR_W), lambda b, s: (0, 0, b, 0))
    cblk = pl.BlockSpec((1, D_MODEL, SB_TK), lambda b, s: (b, 0, n_kt - jnp.maximum(s, 1)))
    return pl.pallas_call(
        functools.partial(_sb_sample_kernel, ts=ts, n_kt=n_kt),
        grid=(bs, n_kt + 1),
        in_specs=[blk, blk, blk, blk, cblk, cblk],
        out_specs=blk,
        out_shape=jax.ShapeDtypeStruct(q.shape, BF16),
        scratch_shapes=[pltpu.VMEM((N_PAIRS, 2 * ts, 1), F32),
                        pltpu.VMEM((N_PAIRS, 2 * ts, PAIR_W), F32)],
        name="sb_sample", compiler_params=_params(("arbitrary", "arbitrary")))(q, k, v, g, cache_kt, cache_vt)


def kernel(x_prompt, x_sample, cache_a_k, cache_a_v, cache_b_k, cache_b_v, norm_a, w_in_a, rel_bias_a,
           w_out_a, norm_kv, w_kv, norm_b, w_in_b, w_out_b, norm_f):
    b, t, d = x_prompt.shape
    bs, ts, _ = x_sample.shape
    past = cache_b_k.shape[1]
    assert d == D_MODEL and norm_a.shape[0] == 1 and norm_b.shape[0] == 1
    assert cache_a_k.shape[2] == A_CACHE_ROWS and t % SB_TQ == 0 and t >= A_CACHE_ROWS
    assert past % CHUNK == 0 and ts == CHUNK

    w_a = w_in_a[0].astype(BF16)
    wo_a = w_out_a[0].astype(BF16)
    w_kvb = w_kv.astype(BF16)
    w_b = w_in_b[0].astype(BF16)
    wo_b = w_out_b[0].astype(BF16)
    g_a = norm_a
    g_b = jnp.stack([norm_kv, norm_b[0]])
    g_f = norm_f[None]
    xs = x_sample.reshape(1, bs * ts, d)

    time_minor = lambda c: jnp.transpose(c, (0, 2, 3, 1)).reshape(c.shape[0], d, c.shape[1])

    bias_p, bias_s = _bias_call(rel_bias_a[0], past, ts)

    plan_a = (Seg(0, 0, 0, SCALE * LOG2E, 0, None), Seg(0, 0, 1, 1.0, 1, 0),
              Seg(0, 0, 2, 1.0, 2, 1), Seg(0, 0, 3, 1.0, 3, None))
    plan_b = (Seg(0, 0, 0, 1.0, 0, 0), Seg(0, 0, 1, 1.0, 1, 1),
              Seg(1, 1, 0, SCALE * LOG2E, 2, None), Seg(1, 1, 1, 1.0, 3, None))

    qp, kp, vp, gp, akp, avp = _dense_call(x_prompt, gains=g_a, weights=(w_a,), plan=plan_a,
                                           f32_last_rows=A_CACHE_ROWS, name="proj_a_prompt")
    qs, ks, vs, gs, aks, avs = _dense_call(xs, gains=g_a, weights=(w_a,), plan=plan_a,
                                           name="proj_a_sample")

    ogp = _band_prompt_call(qp, kp, vp, gp, bias_p)
    ogs = _band_sample_call(qs, ks, vs, gs, time_minor(cache_a_k[0]), time_minor(cache_a_v[0]),
                            bias_s, bs, ts)

    xp1, kbp, vbp, qbp, gbp, kbp32, vbp32 = _dense_call(
        x_prompt, og=ogp, wo=wo_a, gains=g_b, weights=(w_kvb, w_b), plan=plan_b, emit_x=True, tm=256,
        name="out_a_proj_b_prompt")
    xs1, kbs, vbs, qbs, gbs, kbs32, vbs32 = _dense_call(
        xs, og=ogs, wo=wo_a, gains=g_b, weights=(w_kvb, w_b), plan=plan_b, emit_x=True, tm=256,
        name="out_a_proj_b_sample")

    obp = _sb_prompt_call(qbp, kbp, vbp, gbp)
    obs = _sb_sample_call(qbs, kbs, vbs, gbs, time_minor(cache_b_k), time_minor(cache_b_v), bs, ts)

    (y_prompt,) = _dense_call(xp1, og=obp, wo=wo_b, gf=g_f, name="out_b_prompt")
    (y_sample,) = _dense_call(xs1, og=obs, wo=wo_b, gf=g_f, name="out_b_sample")

    heads = lambda a, n, rows: a.reshape(n, rows, N_HEADS, HEAD_DIM)
    return (y_prompt, y_sample.reshape(bs, ts, d),
            heads(akp, b, A_CACHE_ROWS)[None], heads(avp, b, A_CACHE_ROWS)[None],
            heads(kbp32, b, t), heads(vbp32, b, t),
            heads(aks, bs, ts)[None], heads(avs, bs, ts)[None],
            heads(kbs32, bs, ts), heads(vbs32, bs, ts))
```

```python
import functools
from typing import NamedTuple, Optional

import jax
import jax.numpy as jnp
from jax import lax
from jax.experimental import pallas as pl
from jax.experimental.pallas import tpu as pltpu

D_MODEL = 1024
N_HEADS = 16
HEAD_DIM = 64
PAIR_W = 2 * HEAD_DIM
N_PAIRS = N_HEADS // 2
CHUNK = 64
LEFT_CHUNKS = 8
A_CACHE_ROWS = LEFT_CHUNKS * CHUNK
REL_CLIP = 128
N_REL = 2 * REL_CLIP + 1
N_REL_PAD = 384
RMS_EPS = 1e-6
NEG_INF = -1e30
SCALE = HEAD_DIM ** -0.5
LOG2E = 1.4426950408889634

BAND_TQ = 128
BAND_TK = A_CACHE_ROWS + BAND_TQ
BAND_VARIANTS = A_CACHE_ROWS // BAND_TQ + 1
BAND_CHAINS = 16
SB_TQ = 256
SB_TK = 256
SB_CHAINS = 4
SB_SAMPLE_KEYS = 1024
SB_Z_CLAMP = 126.0

VMEM_LIMIT = 56 * 1024 * 1024

F32 = jnp.float32
BF16 = jnp.bfloat16


def _params(sem):
    return pltpu.CompilerParams(dimension_semantics=sem, vmem_limit_bytes=VMEM_LIMIT)


class Seg(NamedTuple):
    norm: int
    w: int
    col: int
    scale: float
    pm: Optional[int]
    f32: Optional[int]


def _dense_kernel(*refs, has_resid, emit_x, n_norm, n_w, plan, final_norm, n_pm, n_f32):
    it = iter(refs)
    x_ref = next(it)
    og_ref = next(it) if has_resid else None
    wo_ref = next(it) if has_resid else None
    g_ref = next(it) if n_norm else None
    w_refs = [next(it) for _ in range(n_w)]
    gf_ref = next(it) if final_norm else None
    xo_ref = next(it) if emit_x else None
    pm_refs = [next(it) for _ in range(n_pm)]
    f32_refs = [next(it) for _ in range(n_f32)]
    y_ref = next(it) if final_norm else None

    x = x_ref[0]
    if has_resid:
        og = jnp.concatenate([og_ref[0, p] for p in range(N_PAIRS)], axis=1)
        x = x + jnp.dot(og, wo_ref[...], preferred_element_type=F32)
        if emit_x:
            xo_ref[0] = x
    if n_norm or final_norm:
        xn = x * lax.rsqrt(jnp.mean(x * x, axis=-1, keepdims=True) + RMS_EPS)
    if final_norm:
        y_ref[0] = xn * gf_ref[...]
    hs = [(xn * g_ref[i:i + 1, :]).astype(BF16) for i in range(n_norm)]
    for seg in plan:
        w = w_refs[seg.w][:, seg.col * D_MODEL:(seg.col + 1) * D_MODEL]
        acc = jnp.dot(hs[seg.norm], w, preferred_element_type=F32)
        if seg.f32 is not None:
            f32_refs[seg.f32][0] = acc
        if seg.pm is not None:
            ab = (acc * seg.scale).astype(BF16) if seg.scale != 1.0 else acc.astype(BF16)
            for p in range(N_PAIRS):
                pm_refs[seg.pm][0, p] = ab[:, p * PAIR_W:(p + 1) * PAIR_W]


def _dense_call(x, *, og=None, wo=None, gains=None, weights=(), plan=(), gf=None,
                emit_x=False, f32_last_rows=None, tm=512, name="dense"):
    bx, tx, d = x.shape
    assert d == D_MODEL and tx % tm == 0
    has_resid = og is not None
    n_norm = 0 if gains is None else gains.shape[0]
    n_pm = sum(s.pm is not None for s in plan)
    n_f32 = sum(s.f32 is not None for s in plan)
    final_norm = gf is not None

    row_spec = pl.BlockSpec((1, tm, d), lambda b, t: (b, t, 0))
    pm_spec = pl.BlockSpec((1, N_PAIRS, tm, PAIR_W), lambda b, t: (b, 0, t, 0))
    whole = lambda a: pl.BlockSpec(a.shape, lambda b, t: (0,) * a.ndim, pipeline_mode=pl.Buffered(1))

    in_arrays, in_specs = [x], [row_spec]
    if has_resid:
        in_arrays += [og, wo]
        in_specs += [pm_spec, whole(wo)]
    if n_norm:
        in_arrays.append(gains)
        in_specs.append(whole(gains))
    for w in weights:
        in_arrays.append(w)
        in_specs.append(whole(w))
    if final_norm:
        in_arrays.append(gf)
        in_specs.append(whole(gf))

    out_shapes, out_specs = [], []
    if emit_x:
        out_shapes.append(jax.ShapeDtypeStruct((bx, tx, d), F32))
        out_specs.append(row_spec)
    for _ in range(n_pm):
        out_shapes.append(jax.ShapeDtypeStruct((bx, N_PAIRS, tx, PAIR_W), BF16))
        out_specs.append(pm_spec)
    for _ in range(n_f32):
        if f32_last_rows is None:
            out_shapes.append(jax.ShapeDtypeStruct((bx, tx, d), F32))
            out_specs.append(row_spec)
        else:
            assert f32_last_rows == tm
            out_shapes.append(jax.ShapeDtypeStruct((bx, tm, d), F32))
            out_specs.append(pl.BlockSpec((1, tm, d), lambda b, t: (b, 0, 0)))
    if final_norm:
        out_shapes.append(jax.ShapeDtypeStruct((bx, tx, d), F32))
        out_specs.append(row_spec)

    body = functools.partial(
        _dense_kernel, has_resid=has_resid, emit_x=emit_x, n_norm=n_norm, n_w=len(weights),
        plan=tuple(plan), final_norm=final_norm, n_pm=n_pm, n_f32=n_f32)
    return pl.pallas_call(
        body, grid=(bx, tx // tm), in_specs=in_specs, out_specs=out_specs, out_shape=out_shapes,
        name=name, compiler_params=_params(("arbitrary", "arbitrary")))(*in_arrays)


def _bias_kernel(rb_ref, bp_ref, bs_ref, *, past_len, ts):
    width = 768
    d0 = A_CACHE_ROWS
    rb = rb_ref[0]
    hi = rb.astype(BF16)
    r1 = rb - hi.astype(F32)
    mid = r1.astype(BF16)
    lo = (r1 - mid.astype(F32)).astype(BF16)
    c = lax.broadcasted_iota(jnp.int32, (N_REL_PAD, width), 0)
    n = lax.broadcasted_iota(jnp.int32, (N_REL_PAD, width), 1)
    m = jnp.where(n < BAND_TK, n, n - width)
    tgt = jnp.clip(d0 - m, -REL_CLIP, REL_CLIP) + REL_CLIP
    onehot = jnp.where(c == tgt, 1.0, 0.0).astype(BF16)
    r_ext = (jnp.dot(hi, onehot, preferred_element_type=F32)
             + jnp.dot(mid, onehot, preferred_element_type=F32)
             + jnp.dot(lo, onehot, preferred_element_type=F32))

    sub = lax.broadcasted_iota(jnp.int32, (8, width), 0)
    nks = A_CACHE_ROWS + ts
    for h2 in range(2):
        base = jnp.broadcast_to(r_ext[h2:h2 + 1, :], (8, width))
        b8 = base
        for r in range(1, 8):
            b8 = jnp.where(sub == r, pltpu.roll(base, r, 1), b8)
        blocks = [b8] + [pltpu.roll(b8, 8 * gi, 1) for gi in range(1, BAND_TQ // 8)]
        toep = jnp.concatenate(blocks, axis=0)[:, :BAND_TK]

        def masked(q0, k0, rows, cols):
            qpos = q0 + lax.broadcasted_iota(jnp.int32, (rows, cols), 0)
            kpos = k0 + lax.broadcasted_iota(jnp.int32, (rows, cols), 1)
            qc0 = qpos - jnp.bitwise_and(qpos, CHUNK - 1)
            lo_k = jnp.maximum(qc0 - A_CACHE_ROWS, 0)
            ok = jnp.logical_and(kpos >= lo_k, kpos < qc0 + CHUNK)
            return jnp.where(ok, toep[:rows, :cols] * LOG2E, NEG_INF)

        for v in range(BAND_VARIANTS):
            q0 = BAND_TQ * v
            bp_ref[0, v, h2 * BAND_TQ:(h2 + 1) * BAND_TQ, :] = masked(q0, q0 - A_CACHE_ROWS, BAND_TQ, BAND_TK)
        bs_ref[0, h2 * ts:(h2 + 1) * ts, :] = masked(past_len, past_len - A_CACHE_ROWS, ts, nks)


def _bias_call(rel_bias, past_len, ts):
    rb = jnp.pad(rel_bias.reshape(N_PAIRS, 2, N_REL), ((0, 0), (0, 6), (0, N_REL_PAD - N_REL)))
    nks = A_CACHE_ROWS + ts
    return pl.pallas_call(
        functools.partial(_bias_kernel, past_len=past_len, ts=ts),
        grid=(N_PAIRS,),
        in_specs=[pl.BlockSpec((1, 8, N_REL_PAD), lambda p: (p, 0, 0))],
        out_specs=[pl.BlockSpec((1, BAND_VARIANTS, 2 * BAND_TQ, BAND_TK), lambda p: (p, 0, 0, 0)),
                   pl.BlockSpec((1, 2 * ts, nks), lambda p: (p, 0, 0))],
        out_shape=[jax.ShapeDtypeStruct((N_PAIRS, BAND_VARIANTS, 2 * BAND_TQ, BAND_TK), F32),
                   jax.ShapeDtypeStruct((N_PAIRS, 2 * ts, nks), F32)],
        name="band_bias", compiler_params=_params(("arbitrary",)))(rb)


def _first_head_lanes(rows):
    return lax.broadcasted_iota(jnp.int32, (rows, PAIR_W), 1) < HEAD_DIM


def _stack_heads(q, first):
    zero = jnp.zeros_like(q)
    return jnp.concatenate([jnp.where(first, q, zero), jnp.where(first, zero, q)], axis=0)


def _unstack_heads(o2, first):
    rows = o2.shape[0] // 2
    return jnp.where(first, o2[:rows], o2[rows:])


def _nt_dot(a, b):
    return lax.dot_general(a, b, (((1,), (1,)), ((), ())), preferred_element_type=F32)


def _gate(o, g):
    g = g.astype(F32)
    return (o * (g * jax.nn.sigmoid(g))).astype(BF16)


def _nn_dot(a, b):
    return jnp.dot(a, b, preferred_element_type=F32)


def _skewed(n_chains, n_stages, stage):
    for slot in range(n_stages + n_chains - 1):
        for c in range(n_chains):
            if 0 <= slot - c < n_stages:
                stage(c, slot - c)


def _softmax_chains(score_fns, pv_fns):
    n = len(score_fns)
    st = [dict() for _ in range(n)]

    def stage(c, s):
        d = st[c]
        if s == 0:
            d["s"] = score_fns[c]()
        elif s == 1:
            d["m"] = jnp.max(d["s"], axis=-1, keepdims=True)
        elif s == 2:
            p = jnp.exp2(d.pop("s") - d.pop("m"))
            d["l"] = jnp.sum(p, axis=-1, keepdims=True)
            d["p"] = p.astype(BF16)
        elif s == 3:
            d["o"] = pv_fns[c](d.pop("p"))
        else:
            d["o"] = d["o"] * (1.0 / d.pop("l"))

    _skewed(n, 5, stage)
    return [d["o"] for d in st]


def _band_prompt_kernel(q_ref, k_ref, v_ref, g_ref, bias_ref, o_ref, kpad, vpad, *, t):
    zeros = jnp.zeros((A_CACHE_ROWS, PAIR_W), BF16)
    kpad[0:A_CACHE_ROWS, :] = zeros
    vpad[0:A_CACHE_ROWS, :] = zeros
    kpad[A_CACHE_ROWS:, :] = k_ref[0, 0]
    vpad[A_CACHE_ROWS:, :] = v_ref[0, 0]
    first = _first_head_lanes(BAND_TQ)

    def body(i, carry):
        qts = [i * BAND_CHAINS + c for c in range(BAND_CHAINS)]
        r0s = [pl.multiple_of(qt * BAND_TQ, BAND_TQ) for qt in qts]
        def score_fn(qt, r0):
            def fn():
                q2 = _stack_heads(q_ref[0, 0, pl.ds(r0, BAND_TQ), :], first)
                kw = kpad[pl.ds(r0, BAND_TK), :]
                return _nt_dot(q2, kw) + bias_ref[0, jnp.minimum(qt, BAND_VARIANTS - 1)]
            return fn

        outs = _softmax_chains(
            [score_fn(qt, r0) for qt, r0 in zip(qts, r0s)],
            [lambda p, r0=r0: _nn_dot(p, vpad[pl.ds(r0, BAND_TK), :]) for r0 in r0s])
        for r0, o2 in zip(r0s, outs):
            o = _unstack_heads(o2, first)
            o_ref[0, 0, pl.ds(r0, BAND_TQ), :] = _gate(o, g_ref[0, 0, pl.ds(r0, BAND_TQ), :])
        return carry

    assert (t // BAND_TQ) % BAND_CHAINS == 0
    lax.fori_loop(0, t // BAND_TQ // BAND_CHAINS, body, 0)


def _band_prompt_call(q, k, v, g, bias_p):
    b, _, t, _ = q.shape
    blk = pl.BlockSpec((1, 1, t, PAIR_W), lambda p, bb: (bb, p, 0, 0))
    return pl.pallas_call(
        functools.partial(_band_prompt_kernel, t=t),
        grid=(N_PAIRS, b),
        in_specs=[blk, blk, blk, blk,
                  pl.BlockSpec((1, BAND_VARIANTS, 2 * BAND_TQ, BAND_TK), lambda p, bb: (p, 0, 0, 0))],
        out_specs=blk,
        out_shape=jax.ShapeDtypeStruct(q.shape, BF16),
        scratch_shapes=[pltpu.VMEM((t + A_CACHE_ROWS, PAIR_W), BF16),
                        pltpu.VMEM((t + A_CACHE_ROWS, PAIR_W), BF16)],
        name="band_prompt", compiler_params=_params(("arbitrary", "arbitrary")))(q, k, v, g, bias_p)


def _band_sample_kernel(q_ref, k_ref, v_ref, g_ref, ckt_ref, cvt_ref, bias_ref, o_ref, *, ts):
    first = _first_head_lanes(ts)
    pairs = range(N_PAIRS)
    rows = [slice(p * PAIR_W, (p + 1) * PAIR_W) for p in pairs]

    def score_fn(p):
        def fn():
            q2 = _stack_heads(q_ref[0, p], first)
            s_old = _nn_dot(q2, ckt_ref[0, rows[p], :].astype(BF16))
            s_new = _nt_dot(q2, k_ref[0, p])
            return jnp.concatenate([s_old, s_new], axis=1) + bias_ref[p]
        return fn

    def pv_fn(p):
        def fn(prob):
            return (_nt_dot(prob[:, :A_CACHE_ROWS], cvt_ref[0, rows[p], :].astype(BF16))
                    + _nn_dot(prob[:, A_CACHE_ROWS:], v_ref[0, p]))
        return fn

    outs = _softmax_chains([score_fn(p) for p in pairs], [pv_fn(p) for p in pairs])
    for p in pairs:
        o_ref[0, p] = _gate(_unstack_heads(outs[p], first), g_ref[0, p])


def _band_sample_call(q, k, v, g, cache_kt, cache_vt, bias_s, bs, ts):
    blk = pl.BlockSpec((1, N_PAIRS, ts, PAIR_W), lambda b: (0, 0, b, 0))
    cblk = pl.BlockSpec((1, D_MODEL, A_CACHE_ROWS), lambda b: (b, 0, 0))
    return pl.pallas_call(
        functools.partial(_band_sample_kernel, ts=ts),
        grid=(bs,),
        in_specs=[blk, blk, blk, blk, cblk, cblk,
                  pl.BlockSpec(bias_s.shape, lambda b: (0, 0, 0))],
        out_specs=blk,
        out_shape=jax.ShapeDtypeStruct(q.shape, BF16),
        name="band_sample", compiler_params=_params(("arbitrary",)))(q, k, v, g, cache_kt, cache_vt, bias_s)


def _suffix_matrix(n):
    r = lax.broadcasted_iota(jnp.int32, (n, n), 0)
    c = lax.broadcasted_iota(jnp.int32, (n, n), 1)
    return jnp.where(r > c, 1.0, 0.0).astype(BF16)


class SbChain(NamedTuple):
    score_fn: object
    pv_fn: object
    suffix: jax.Array
    causal: Optional[jax.Array] = None
    prev: Optional[int] = None
    carry: Optional[jax.Array] = None
    acc: Optional[jax.Array] = None


def _sb_tiles(chains):
    st = [dict() for _ in chains]

    def stage(c, s):
        ch, d = chains[c], st[c]
        if s == 0:
            d["z"] = ch.score_fn()
        elif s == 1:
            z = d.pop("z")
            sp = jnp.maximum(jnp.log(1.0 + jnp.exp2(jnp.minimum(z, SB_Z_CLAMP))) * LOG2E, z)
            if ch.causal is not None:
                sp = jnp.where(ch.causal, sp, 0.0)
            total = jnp.sum(sp, axis=-1, keepdims=True)
            carry = ch.carry if ch.prev is None else st[ch.prev]["carry"]
            d["sp"] = sp.astype(BF16)
            d["zs"] = z - sp if carry is None else (z - sp) - carry
            d["carry"] = total if carry is None else carry + total
        elif s == 2:
            d["later"] = _nn_dot(d.pop("sp"), ch.suffix)
        elif s == 3:
            w = jnp.exp2(d.pop("zs") - d.pop("later"))
            if ch.causal is not None:
                w = jnp.where(ch.causal, w, 0.0)
            d["w"] = w.astype(BF16)
        else:
            pv = ch.pv_fn(d.pop("w"))
            acc = ch.acc if ch.prev is None else st[ch.prev]["acc"]
            d["acc"] = pv if acc is None else acc + pv

    _skewed(len(chains), 5, stage)
    return [(d["carry"], d["acc"]) for d in st]


def _sb_prompt_kernel(q_ref, k_ref, v_ref, g_ref, o_ref, *, t):
    first = _first_head_lanes(SB_TQ)
    suffix = _suffix_matrix(SB_TK)
    row = lax.broadcasted_iota(jnp.int32, (SB_TQ, SB_TK), 0)
    col = lax.broadcasted_iota(jnp.int32, (SB_TQ, SB_TK), 1)
    causal = col < row

    pairs = range(SB_CHAINS)
    chain_pair = [p for p in pairs for _ in range(2)]

    n = len(chain_pair)

    def q_body(qt, c0):
        r0 = pl.multiple_of(qt * SB_TQ, SB_TQ)
        qh = []
        for p in pairs:
            q = q_ref[0, p, pl.ds(r0, SB_TQ), :]
            zero = jnp.zeros_like(q)
            qh += [jnp.where(first, q, zero), jnp.where(first, zero, q)]

        def tile(k0, mask=None, state=None, prev0=None):
            return [SbChain(lambda c=c, p=p: _nt_dot(qh[c], k_ref[0, p, pl.ds(k0, SB_TK), :]),
                            lambda w, p=p: _nn_dot(w, v_ref[0, p, pl.ds(k0, SB_TK), :]),
                            suffix, mask, None if prev0 is None else prev0 + c,
                            None if state is None else state[2 * c],
                            None if state is None else state[2 * c + 1])
                    for c, p in enumerate(chain_pair)]

        def flat(results):
            return tuple(x for r in results for x in r)

        def own_tile_alone():
            return flat(_sb_tiles(tile(r0, causal)))

        def own_tile_and_next():
            k1 = pl.multiple_of(r0 - SB_TK, SB_TK)
            return flat(_sb_tiles(tile(r0, causal) + tile(k1, prev0=0))[n:])

        odd = jnp.bitwise_and(qt, 1)
        state = lax.cond(odd == 1, own_tile_and_next, own_tile_alone)

        def k_body(i, st):
            ka = pl.multiple_of((qt - 1 - odd - 2 * i) * SB_TK, SB_TK)
            kb = pl.multiple_of(ka - SB_TK, SB_TK)
            return flat(_sb_tiles(tile(ka, state=st) + tile(kb, prev0=0))[n:])

        state = lax.fori_loop(0, lax.shift_right_logical(qt, 1), k_body, state)
        for p in pairs:
            o = jnp.where(first, state[4 * p + 1], state[4 * p + 3])
            o_ref[0, p, pl.ds(r0, SB_TQ), :] = _gate(o, g_ref[0, p, pl.ds(r0, SB_TQ), :])
        return c0

    lax.fori_loop(0, t // SB_TQ, q_body, 0)


def _sb_prompt_call(q, k, v, g):
    b, _, t, _ = q.shape
    blk = pl.BlockSpec((1, SB_CHAINS, t, PAIR_W), lambda bb, p: (bb, p, 0, 0))
    return pl.pallas_call(
        functools.partial(_sb_prompt_kernel, t=t),
        grid=(b, N_PAIRS // SB_CHAINS),
        in_specs=[blk, blk, blk, blk],
        out_specs=blk,
        out_shape=jax.ShapeDtypeStruct(q.shape, BF16),
        name="sb_prompt", compiler_params=_params(("arbitrary", "arbitrary")))(q, k, v, g)


def _sb_sample_kernel(q_ref, k_ref, v_ref, g_ref, ckt_ref, cvt_ref, o_ref, carry_ref, acc_ref, *, ts, n_steps):
    step = pl.program_id(1)
    first = _first_head_lanes(ts)
    pairs = range(N_PAIRS)
    rows = [slice(p * PAIR_W, (p + 1) * PAIR_W) for p in pairs]
    q2 = [_stack_heads(q_ref[0, p], first) for p in pairs]
    suffix = _suffix_matrix(SB_TK)

    def cached_tiles(chains, state=None):
        for i in reversed(range(SB_SAMPLE_KEYS // SB_TK)):
            keys = slice(i * SB_TK, (i + 1) * SB_TK)
            base = len(chains) - N_PAIRS
            chains = chains + [
                SbChain(lambda p=p, keys=keys: _nn_dot(q2[p], ckt_ref[0, rows[p], keys].astype(BF16)),
                        lambda w, p=p, keys=keys: _nt_dot(w, cvt_ref[0, rows[p], keys].astype(BF16)),
                        suffix, None, base + p if base >= 0 else None,
                        None if base >= 0 else state[p][0], None if base >= 0 else state[p][1])
                for p in pairs]
        return chains

    def run(chains):
        out = _sb_tiles(chains)[-N_PAIRS:]
        for p in pairs:
            carry_ref[p], acc_ref[p] = out[p]

    @pl.when(step == 0)
    def _():
        row = lax.broadcasted_iota(jnp.int32, (2 * ts, ts), 0)
        col = lax.broadcasted_iota(jnp.int32, (2 * ts, ts), 1)
        causal = col < jnp.bitwise_and(row, ts - 1)
        new_rows = [SbChain(lambda p=p: _nt_dot(q2[p], k_ref[0, p]), lambda w, p=p: _nn_dot(w, v_ref[0, p]),
                            _suffix_matrix(ts), causal) for p in pairs]
        run(cached_tiles(new_rows))

    @pl.when(step > 0)
    def _():
        run(cached_tiles([], [(carry_ref[p], acc_ref[p]) for p in pairs]))

    @pl.when(step == n_steps - 1)
    def _():
        for p in range(N_PAIRS):
            o_ref[0, p] = _gate(_unstack_heads(acc_ref[p], first), g_ref[0, p])


def _sb_sample_call(q, k, v, g, cache_kt, cache_vt, bs, ts):
    past = cache_kt.shape[2]
    assert past % SB_SAMPLE_KEYS == 0
    n_steps = past // SB_SAMPLE_KEYS
    blk = pl.BlockSpec((1, N_PAIRS, ts, PAIR_W), lambda b, s: (0, 0, b, 0))
    cblk = pl.BlockSpec((1, D_MODEL, SB_SAMPLE_KEYS), lambda b, s: (b, 0, n_steps - 1 - s))
    return pl.pallas_call(
        functools.partial(_sb_sample_kernel, ts=ts, n_steps=n_steps),
        grid=(bs, n_steps),
        in_specs=[blk, blk, blk, blk, cblk, cblk],
        out_specs=blk,
        out_shape=jax.ShapeDtypeStruct(q.shape, BF16),
        scratch_shapes=[pltpu.VMEM((N_PAIRS, 2 * ts, 1), F32),
                        pltpu.VMEM((N_PAIRS, 2 * ts, PAIR_W), F32)],
        name="sb_sample", compiler_params=_params(("arbitrary", "arbitrary")))(q, k, v, g, cache_kt, cache_vt)


def kernel(x_prompt, x_sample, cache_a_k, cache_a_v, cache_b_k, cache_b_v, norm_a, w_in_a, rel_bias_a,
           w_out_a, norm_kv, w_kv, norm_b, w_in_b, w_out_b, norm_f):
    b, t, d = x_prompt.shape
    bs, ts, _ = x_sample.shape
    past = cache_b_k.shape[1]
    assert d == D_MODEL and norm_a.shape[0] == 1 and norm_b.shape[0] == 1
    assert cache_a_k.shape[2] == A_CACHE_ROWS and t % SB_TQ == 0 and t >= A_CACHE_ROWS
    assert past % CHUNK == 0 and ts == CHUNK

    w_a = w_in_a[0].astype(BF16)
    wo_a = w_out_a[0].astype(BF16)
    w_kvb = w_kv.astype(BF16)
    w_b = w_in_b[0].astype(BF16)
    wo_b = w_out_b[0].astype(BF16)
    g_a = norm_a
    g_b = jnp.stack([norm_kv, norm_b[0]])
    g_f = norm_f[None]
    xs = x_sample.reshape(1, bs * ts, d)

    time_minor = lambda c: jnp.transpose(c, (0, 2, 3, 1)).reshape(c.shape[0], d, c.shape[1])

    bias_p, bias_s = _bias_call(rel_bias_a[0], past, ts)

    plan_a = (Seg(0, 0, 0, SCALE * LOG2E, 0, None), Seg(0, 0, 1, 1.0, 1, 0),
              Seg(0, 0, 2, 1.0, 2, 1), Seg(0, 0, 3, 1.0, 3, None))
    plan_b = (Seg(0, 0, 0, 1.0, 0, 0), Seg(0, 0, 1, 1.0, 1, 1),
              Seg(1, 1, 0, SCALE * LOG2E, 2, None), Seg(1, 1, 1, 1.0, 3, None))

    qp, kp, vp, gp, akp, avp = _dense_call(x_prompt, gains=g_a, weights=(w_a,), plan=plan_a,
                                           f32_last_rows=A_CACHE_ROWS, name="proj_a_prompt")
    qs, ks, vs, gs, aks, avs = _dense_call(xs, gains=g_a, weights=(w_a,), plan=plan_a,
                                           name="proj_a_sample")

    ogp = _band_prompt_call(qp, kp, vp, gp, bias_p)
    ogs = _band_sample_call(qs, ks, vs, gs, time_minor(cache_a_k[0]), time_minor(cache_a_v[0]),
                            bias_s, bs, ts)

    xp1, kbp, vbp, qbp, gbp, kbp32, vbp32 = _dense_call(
        x_prompt, og=ogp, wo=wo_a, gains=g_b, weights=(w_kvb, w_b), plan=plan_b, emit_x=True,
        name="out_a_proj_b_prompt")
    xs1, kbs, vbs, qbs, gbs, kbs32, vbs32 = _dense_call(
        xs, og=ogs, wo=wo_a, gains=g_b, weights=(w_kvb, w_b), plan=plan_b, emit_x=True,
        name="out_a_proj_b_sample")

    obp = _sb_prompt_call(qbp, kbp, vbp, gbp)
    obs = _sb_sample_call(qbs, kbs, vbs, gbs, time_minor(cache_b_k), time_minor(cache_b_v), bs, ts)

    (y_prompt,) = _dense_call(xp1, og=obp, wo=wo_b, gf=g_f, name="out_b_prompt")
    (y_sample,) = _dense_call(xs1, og=obs, wo=wo_b, gf=g_f, name="out_b_sample")

    heads = lambda a, n, rows: a.reshape(n, rows, N_HEADS, HEAD_DIM)
    return (y_prompt, y_sample.reshape(bs, ts, d),
            heads(akp, b, A_CACHE_ROWS)[None], heads(avp, b, A_CACHE_ROWS)[None],
            heads(kbp32, b, t), heads(vbp32, b, t),
            heads(aks, bs, ts)[None], heads(avs, bs, ts)[None],
            heads(kbs32, bs, ts), heads(vbs32, bs, ts))
```

```python
import functools
from typing import NamedTuple, Optional

import jax
import jax.numpy as jnp
from jax import lax
from jax.experimental import pallas as pl
from jax.experimental.pallas import tpu as pltpu

D_MODEL = 1024
N_HEADS = 16
HEAD_DIM = 64
PAIR_W = 2 * HEAD_DIM
N_PAIRS = N_HEADS // 2
CHUNK = 64
LEFT_CHUNKS = 8
A_CACHE_ROWS = LEFT_CHUNKS * CHUNK
REL_CLIP = 128
N_REL = 2 * REL_CLIP + 1
N_REL_PAD = 384
RMS_EPS = 1e-6
NEG_INF = -1e30
SCALE = HEAD_DIM ** -0.5
LOG2E = 1.4426950408889634

BAND_TQ = 128
BAND_TK = A_CACHE_ROWS + BAND_TQ
BAND_VARIANTS = A_CACHE_ROWS // BAND_TQ + 1
BAND_SAMPLE_BATCH = 2
BAND_CHAINS = 16
SB_TQ = 256
SB_TK = 256
SB_CHAINS = 4
SB_SAMPLE_KEYS = 2048
SB_Z_CLAMP = 126.0

VMEM_LIMIT = 56 * 1024 * 1024

F32 = jnp.float32
BF16 = jnp.bfloat16


def _params(sem):
    return pltpu.CompilerParams(dimension_semantics=sem, vmem_limit_bytes=VMEM_LIMIT)


class Seg(NamedTuple):
    norm: int
    w: int
    col: int
    scale: float
    pm: Optional[int]
    f32: Optional[int]


def _dense_kernel(*refs, has_resid, emit_x, n_norm, n_w, plan, final_norm, n_pm, n_f32):
    it = iter(refs)
    x_ref = next(it)
    og_ref = next(it) if has_resid else None
    wo_ref = next(it) if has_resid else None
    g_ref = next(it) if n_norm else None
    w_refs = [next(it) for _ in range(n_w)]
    gf_ref = next(it) if final_norm else None
    xo_ref = next(it) if emit_x else None
    pm_refs = [next(it) for _ in range(n_pm)]
    f32_refs = [next(it) for _ in range(n_f32)]
    y_ref = next(it) if final_norm else None

    x = x_ref[0]
    if has_resid:
        og = jnp.concatenate([og_ref[0, p] for p in range(N_PAIRS)], axis=1)
        x = x + jnp.dot(og, wo_ref[...], preferred_element_type=F32)
        if emit_x:
            xo_ref[0] = x
    if n_norm or final_norm:
        xn = x * lax.rsqrt(jnp.mean(x * x, axis=-1, keepdims=True) + RMS_EPS)
    if final_norm:
        y_ref[0] = xn * gf_ref[...]
    hs = [(xn * g_ref[i:i + 1, :]).astype(BF16) for i in range(n_norm)]
    for seg in plan:
        w = w_refs[seg.w][:, seg.col * D_MODEL:(seg.col + 1) * D_MODEL]
        acc = jnp.dot(hs[seg.norm], w, preferred_element_type=F32)
        if seg.f32 is not None:
            f32_refs[seg.f32][0] = acc
        if seg.pm is not None:
            ab = (acc * seg.scale).astype(BF16) if seg.scale != 1.0 else acc.astype(BF16)
            for p in range(N_PAIRS):
                pm_refs[seg.pm][0, p] = ab[:, p * PAIR_W:(p + 1) * PAIR_W]


def _dense_call(x, *, og=None, wo=None, gains=None, weights=(), plan=(), gf=None,
                emit_x=False, f32_last_rows=None, tm=512, name="dense"):
    bx, tx, d = x.shape
    assert d == D_MODEL and tx % tm == 0
    has_resid = og is not None
    n_norm = 0 if gains is None else gains.shape[0]
    n_pm = sum(s.pm is not None for s in plan)
    n_f32 = sum(s.f32 is not None for s in plan)
    final_norm = gf is not None

    row_spec = pl.BlockSpec((1, tm, d), lambda b, t: (b, t, 0))
    pm_spec = pl.BlockSpec((1, N_PAIRS, tm, PAIR_W), lambda b, t: (b, 0, t, 0))
    whole = lambda a: pl.BlockSpec(a.shape, lambda b, t: (0,) * a.ndim, pipeline_mode=pl.Buffered(1))

    in_arrays, in_specs = [x], [row_spec]
    if has_resid:
        in_arrays += [og, wo]
        in_specs += [pm_spec, whole(wo)]
    if n_norm:
        in_arrays.append(gains)
        in_specs.append(whole(gains))
    for w in weights:
        in_arrays.append(w)
        in_specs.append(whole(w))
    if final_norm:
        in_arrays.append(gf)
        in_specs.append(whole(gf))

    out_shapes, out_specs = [], []
    if emit_x:
        out_shapes.append(jax.ShapeDtypeStruct((bx, tx, d), F32))
        out_specs.append(row_spec)
    for _ in range(n_pm):
        out_shapes.append(jax.ShapeDtypeStruct((bx, N_PAIRS, tx, PAIR_W), BF16))
        out_specs.append(pm_spec)
    for _ in range(n_f32):
        if f32_last_rows is None:
            out_shapes.append(jax.ShapeDtypeStruct((bx, tx, d), F32))
            out_specs.append(row_spec)
        else:
            assert f32_last_rows == tm
            out_shapes.append(jax.ShapeDtypeStruct((bx, tm, d), F32))
            out_specs.append(pl.BlockSpec((1, tm, d), lambda b, t: (b, 0, 0)))
    if final_norm:
        out_shapes.append(jax.ShapeDtypeStruct((bx, tx, d), F32))
        out_specs.append(row_spec)

    body = functools.partial(
        _dense_kernel, has_resid=has_resid, emit_x=emit_x, n_norm=n_norm, n_w=len(weights),
        plan=tuple(plan), final_norm=final_norm, n_pm=n_pm, n_f32=n_f32)
    return pl.pallas_call(
        body, grid=(bx, tx // tm), in_specs=in_specs, out_specs=out_specs, out_shape=out_shapes,
        name=name, compiler_params=_params(("arbitrary", "arbitrary")))(*in_arrays)


def _bias_kernel(rb_ref, bp_ref, bs_ref, *, past_len, ts):
    width = 768
    d0 = A_CACHE_ROWS
    rb = rb_ref[0]
    hi = rb.astype(BF16)
    r1 = rb - hi.astype(F32)
    mid = r1.astype(BF16)
    lo = (r1 - mid.astype(F32)).astype(BF16)
    c = lax.broadcasted_iota(jnp.int32, (N_REL_PAD, width), 0)
    n = lax.broadcasted_iota(jnp.int32, (N_REL_PAD, width), 1)
    m = jnp.where(n < BAND_TK, n, n - width)
    tgt = jnp.clip(d0 - m, -REL_CLIP, REL_CLIP) + REL_CLIP
    onehot = jnp.where(c == tgt, 1.0, 0.0).astype(BF16)
    r_ext = (jnp.dot(hi, onehot, preferred_element_type=F32)
             + jnp.dot(mid, onehot, preferred_element_type=F32)
             + jnp.dot(lo, onehot, preferred_element_type=F32))

    sub = lax.broadcasted_iota(jnp.int32, (8, width), 0)
    nks = A_CACHE_ROWS + ts
    for h2 in range(2):
        base = jnp.broadcast_to(r_ext[h2:h2 + 1, :], (8, width))
        b8 = base
        for r in range(1, 8):
            b8 = jnp.where(sub == r, pltpu.roll(base, r, 1), b8)
        blocks = [b8] + [pltpu.roll(b8, 8 * gi, 1) for gi in range(1, BAND_TQ // 8)]
        toep = jnp.concatenate(blocks, axis=0)[:, :BAND_TK]

        def masked(q0, k0, rows, cols):
            qpos = q0 + lax.broadcasted_iota(jnp.int32, (rows, cols), 0)
            kpos = k0 + lax.broadcasted_iota(jnp.int32, (rows, cols), 1)
            qc0 = qpos - jnp.bitwise_and(qpos, CHUNK - 1)
            lo_k = jnp.maximum(qc0 - A_CACHE_ROWS, 0)
            ok = jnp.logical_and(kpos >= lo_k, kpos < qc0 + CHUNK)
            return jnp.where(ok, toep[:rows, :cols] * LOG2E, NEG_INF)

        for v in range(BAND_VARIANTS):
            q0 = BAND_TQ * v
            bp_ref[0, v, h2 * BAND_TQ:(h2 + 1) * BAND_TQ, :] = masked(q0, q0 - A_CACHE_ROWS, BAND_TQ, BAND_TK)
        bs_ref[0, h2 * ts:(h2 + 1) * ts, :] = masked(past_len, past_len - A_CACHE_ROWS, ts, nks)


def _bias_call(rel_bias, past_len, ts):
    rb = jnp.pad(rel_bias.reshape(N_PAIRS, 2, N_REL), ((0, 0), (0, 6), (0, N_REL_PAD - N_REL)))
    nks = A_CACHE_ROWS + ts
    return pl.pallas_call(
        functools.partial(_bias_kernel, past_len=past_len, ts=ts),
        grid=(N_PAIRS,),
        in_specs=[pl.BlockSpec((1, 8, N_REL_PAD), lambda p: (p, 0, 0))],
        out_specs=[pl.BlockSpec((1, BAND_VARIANTS, 2 * BAND_TQ, BAND_TK), lambda p: (p, 0, 0, 0)),
                   pl.BlockSpec((1, 2 * ts, nks), lambda p: (p, 0, 0))],
        out_shape=[jax.ShapeDtypeStruct((N_PAIRS, BAND_VARIANTS, 2 * BAND_TQ, BAND_TK), F32),
                   jax.ShapeDtypeStruct((N_PAIRS, 2 * ts, nks), F32)],
        name="band_bias", compiler_params=_params(("arbitrary",)))(rb)


def _first_head_lanes(rows):
    return lax.broadcasted_iota(jnp.int32, (rows, PAIR_W), 1) < HEAD_DIM


def _stack_heads(q, first):
    zero = jnp.zeros_like(q)
    return jnp.concatenate([jnp.where(first, q, zero), jnp.where(first, zero, q)], axis=0)


def _unstack_heads(o2, first):
    rows = o2.shape[0] // 2
    return jnp.where(first, o2[:rows], o2[rows:])


def _nt_dot(a, b):
    return lax.dot_general(a, b, (((1,), (1,)), ((), ())), preferred_element_type=F32)


def _gate(o, g):
    g = g.astype(F32)
    return (o * (g * jax.nn.sigmoid(g))).astype(BF16)


def _nn_dot(a, b):
    return jnp.dot(a, b, preferred_element_type=F32)


def _skewed(n_chains, n_stages, stage):
    for slot in range(n_stages + n_chains - 1):
        for c in range(n_chains):
            if 0 <= slot - c < n_stages:
                stage(c, slot - c)


def _softmax_chains(score_fns, pv_fns):
    n = len(score_fns)
    st = [dict() for _ in range(n)]

    def stage(c, s):
        d = st[c]
        if s == 0:
            d["s"] = score_fns[c]()
        elif s == 1:
            d["m"] = jnp.max(d["s"], axis=-1, keepdims=True)
        elif s == 2:
            p = jnp.exp2(d.pop("s") - d.pop("m"))
            d["l"] = jnp.sum(p, axis=-1, keepdims=True)
            d["p"] = p.astype(BF16)
        elif s == 3:
            d["o"] = pv_fns[c](d.pop("p"))
        else:
            d["o"] = d["o"] * (1.0 / d.pop("l"))

    _skewed(n, 5, stage)
    return [d["o"] for d in st]


def _band_prompt_kernel(q_ref, k_ref, v_ref, g_ref, bias_ref, o_ref, kpad, vpad, *, t):
    zeros = jnp.zeros((A_CACHE_ROWS, PAIR_W), BF16)
    kpad[0:A_CACHE_ROWS, :] = zeros
    vpad[0:A_CACHE_ROWS, :] = zeros
    kpad[A_CACHE_ROWS:, :] = k_ref[0, 0]
    vpad[A_CACHE_ROWS:, :] = v_ref[0, 0]
    first = _first_head_lanes(BAND_TQ)

    def body(i, carry):
        qts = [i * BAND_CHAINS + c for c in range(BAND_CHAINS)]
        r0s = [pl.multiple_of(qt * BAND_TQ, BAND_TQ) for qt in qts]
        def score_fn(qt, r0):
            def fn():
                q2 = _stack_heads(q_ref[0, 0, pl.ds(r0, BAND_TQ), :], first)
                kw = kpad[pl.ds(r0, BAND_TK), :]
                return _nt_dot(q2, kw) + bias_ref[0, jnp.minimum(qt, BAND_VARIANTS - 1)]
            return fn

        outs = _softmax_chains(
            [score_fn(qt, r0) for qt, r0 in zip(qts, r0s)],
            [lambda p, r0=r0: _nn_dot(p, vpad[pl.ds(r0, BAND_TK), :]) for r0 in r0s])
        for r0, o2 in zip(r0s, outs):
            o = _unstack_heads(o2, first)
            o_ref[0, 0, pl.ds(r0, BAND_TQ), :] = _gate(o, g_ref[0, 0, pl.ds(r0, BAND_TQ), :])
        return carry

    assert (t // BAND_TQ) % BAND_CHAINS == 0
    lax.fori_loop(0, t // BAND_TQ // BAND_CHAINS, body, 0)


def _band_prompt_call(q, k, v, g, bias_p):
    b, _, t, _ = q.shape
    blk = pl.BlockSpec((1, 1, t, PAIR_W), lambda p, bb: (bb, p, 0, 0))
    return pl.pallas_call(
        functools.partial(_band_prompt_kernel, t=t),
        grid=(N_PAIRS, b),
        in_specs=[blk, blk, blk, blk,
                  pl.BlockSpec((1, BAND_VARIANTS, 2 * BAND_TQ, BAND_TK), lambda p, bb: (p, 0, 0, 0))],
        out_specs=blk,
        out_shape=jax.ShapeDtypeStruct(q.shape, BF16),
        scratch_shapes=[pltpu.VMEM((t + A_CACHE_ROWS, PAIR_W), BF16),
                        pltpu.VMEM((t + A_CACHE_ROWS, PAIR_W), BF16)],
        name="band_prompt", compiler_params=_params(("arbitrary", "arbitrary")))(q, k, v, g, bias_p)


def _band_sample_kernel(q_ref, k_ref, v_ref, g_ref, ckt_ref, cvt_ref, bias_ref, o_ref, *, ts):
    first = _first_head_lanes(ts)
    units = [(i, p) for i in range(BAND_SAMPLE_BATCH) for p in range(N_PAIRS)]
    rows = lambda p: slice(p * PAIR_W, (p + 1) * PAIR_W)
    new = lambda i: slice(i * ts, (i + 1) * ts)

    def score_fn(i, p):
        def fn():
            q2 = _stack_heads(q_ref[0, p, new(i)], first)
            s_old = _nn_dot(q2, ckt_ref[i, rows(p), :].astype(BF16))
            s_new = _nt_dot(q2, k_ref[0, p, new(i)])
            return jnp.concatenate([s_old, s_new], axis=1) + bias_ref[p]
        return fn

    def pv_fn(i, p):
        def fn(prob):
            return (_nt_dot(prob[:, :A_CACHE_ROWS], cvt_ref[i, rows(p), :].astype(BF16))
                    + _nn_dot(prob[:, A_CACHE_ROWS:], v_ref[0, p, new(i)]))
        return fn

    outs = _softmax_chains([score_fn(i, p) for i, p in units], [pv_fn(i, p) for i, p in units])
    for (i, p), o2 in zip(units, outs):
        o_ref[0, p, new(i)] = _gate(_unstack_heads(o2, first), g_ref[0, p, new(i)])


def _band_sample_call(q, k, v, g, cache_kt, cache_vt, bias_s, bs, ts):
    nb = BAND_SAMPLE_BATCH
    assert bs % nb == 0
    blk = pl.BlockSpec((1, N_PAIRS, nb * ts, PAIR_W), lambda b: (0, 0, b, 0))
    cblk = pl.BlockSpec((nb, D_MODEL, A_CACHE_ROWS), lambda b: (b, 0, 0))
    return pl.pallas_call(
        functools.partial(_band_sample_kernel, ts=ts),
        grid=(bs // nb,),
        in_specs=[blk, blk, blk, blk, cblk, cblk,
                  pl.BlockSpec(bias_s.shape, lambda b: (0, 0, 0))],
        out_specs=blk,
        out_shape=jax.ShapeDtypeStruct(q.shape, BF16),
        name="band_sample", compiler_params=_params(("arbitrary",)))(q, k, v, g, cache_kt, cache_vt, bias_s)


def _suffix_matrix(n):
    r = lax.broadcasted_iota(jnp.int32, (n, n), 0)
    c = lax.broadcasted_iota(jnp.int32, (n, n), 1)
    return jnp.where(r > c, 1.0, 0.0).astype(BF16)


class SbChain(NamedTuple):
    score_fn: object
    pv_fn: object
    suffix: jax.Array
    causal: Optional[jax.Array] = None
    prev: Optional[int] = None
    carry: Optional[jax.Array] = None
    acc: Optional[jax.Array] = None


def _sb_tiles(chains):
    st = [dict() for _ in chains]

    def stage(c, s):
        ch, d = chains[c], st[c]
        if s == 0:
            d["z"] = ch.score_fn()
        elif s == 1:
            z = d.pop("z")
            sp = jnp.maximum(jnp.log2(1.0 + jnp.exp2(jnp.minimum(z, SB_Z_CLAMP))), z)
            if ch.causal is not None:
                sp = jnp.where(ch.causal, sp, 0.0)
            total = jnp.sum(sp, axis=-1, keepdims=True)
            carry = ch.carry if ch.prev is None else st[ch.prev]["carry"]
            d["sp"] = sp.astype(BF16)
            d["zs"] = z - sp if carry is None else (z - sp) - carry
            d["carry"] = total if carry is None else carry + total
        elif s == 2:
            d["later"] = _nn_dot(d.pop("sp"), ch.suffix)
        elif s == 3:
            w = jnp.exp2(d.pop("zs") - d.pop("later"))
            if ch.causal is not None:
                w = jnp.where(ch.causal, w, 0.0)
            d["w"] = w.astype(BF16)
        else:
            pv = ch.pv_fn(d.pop("w"))
            acc = ch.acc if ch.prev is None else st[ch.prev]["acc"]
            d["acc"] = pv if acc is None else acc + pv

    _skewed(len(chains), 5, stage)
    return [(d["carry"], d["acc"]) for d in st]


def _sb_prompt_kernel(q_ref, k_ref, v_ref, g_ref, o_ref, *, t):
    first = _first_head_lanes(SB_TQ)
    suffix = _suffix_matrix(SB_TK)
    row = lax.broadcasted_iota(jnp.int32, (SB_TQ, SB_TK), 0)
    col = lax.broadcasted_iota(jnp.int32, (SB_TQ, SB_TK), 1)
    causal = col < row

    pairs = range(SB_CHAINS)
    chain_pair = [p for p in pairs for _ in range(2)]

    n = len(chain_pair)

    def q_body(qt, c0):
        r0 = pl.multiple_of(qt * SB_TQ, SB_TQ)
        qh = []
        for p in pairs:
            q = q_ref[0, p, pl.ds(r0, SB_TQ), :]
            zero = jnp.zeros_like(q)
            qh += [jnp.where(first, q, zero), jnp.where(first, zero, q)]

        def tile(k0, mask=None, state=None, prev0=None):
            return [SbChain(lambda c=c, p=p: _nt_dot(qh[c], k_ref[0, p, pl.ds(k0, SB_TK), :]),
                            lambda w, p=p: _nn_dot(w, v_ref[0, p, pl.ds(k0, SB_TK), :]),
                            suffix, mask, None if prev0 is None else prev0 + c,
                            None if state is None else state[2 * c],
                            None if state is None else state[2 * c + 1])
                    for c, p in enumerate(chain_pair)]

        def flat(results):
            return tuple(x for r in results for x in r)

        def own_tile_alone():
            return flat(_sb_tiles(tile(r0, causal)))

        def own_tile_and_next():
            k1 = pl.multiple_of(r0 - SB_TK, SB_TK)
            return flat(_sb_tiles(tile(r0, causal) + tile(k1, prev0=0))[n:])

        odd = jnp.bitwise_and(qt, 1)
        state = lax.cond(odd == 1, own_tile_and_next, own_tile_alone)

        def k_body(i, st):
            ka = pl.multiple_of((qt - 1 - odd - 2 * i) * SB_TK, SB_TK)
            kb = pl.multiple_of(ka - SB_TK, SB_TK)
            return flat(_sb_tiles(tile(ka, state=st) + tile(kb, prev0=0))[n:])

        state = lax.fori_loop(0, lax.shift_right_logical(qt, 1), k_body, state)
        for p in pairs:
            o = jnp.where(first, state[4 * p + 1], state[4 * p + 3])
            o_ref[0, p, pl.ds(r0, SB_TQ), :] = _gate(o, g_ref[0, p, pl.ds(r0, SB_TQ), :])
        return c0

    lax.fori_loop(0, t // SB_TQ, q_body, 0)


def _sb_prompt_call(q, k, v, g):
    b, _, t, _ = q.shape
    blk = pl.BlockSpec((1, SB_CHAINS, t, PAIR_W), lambda bb, p: (bb, p, 0, 0))
    return pl.pallas_call(
        functools.partial(_sb_prompt_kernel, t=t),
        grid=(b, N_PAIRS // SB_CHAINS),
        in_specs=[blk, blk, blk, blk],
        out_specs=blk,
        out_shape=jax.ShapeDtypeStruct(q.shape, BF16),
        name="sb_prompt", compiler_params=_params(("arbitrary", "arbitrary")))(q, k, v, g)


def _sb_sample_kernel(q_ref, k_ref, v_ref, g_ref, ckt_ref, cvt_ref, o_ref, carry_ref, acc_ref, *, ts, n_steps):
    step = pl.program_id(1)
    first = _first_head_lanes(ts)
    pairs = range(N_PAIRS)
    rows = [slice(p * PAIR_W, (p + 1) * PAIR_W) for p in pairs]
    q2 = [_stack_heads(q_ref[0, p], first) for p in pairs]
    suffix = _suffix_matrix(SB_TK)

    def cached_tiles(chains, state=None):
        for i in reversed(range(SB_SAMPLE_KEYS // SB_TK)):
            keys = slice(i * SB_TK, (i + 1) * SB_TK)
            base = len(chains) - N_PAIRS
            chains = chains + [
                SbChain(lambda p=p, keys=keys: _nn_dot(q2[p], ckt_ref[0, rows[p], keys].astype(BF16)),
                        lambda w, p=p, keys=keys: _nt_dot(w, cvt_ref[0, rows[p], keys].astype(BF16)),
                        suffix, None, base + p if base >= 0 else None,
                        None if base >= 0 else state[p][0], None if base >= 0 else state[p][1])
                for p in pairs]
        return chains

    def run(chains):
        out = _sb_tiles(chains)[-N_PAIRS:]
        for p in pairs:
            carry_ref[p], acc_ref[p] = out[p]

    @pl.when(step == 0)
    def _():
        row = lax.broadcasted_iota(jnp.int32, (2 * ts, ts), 0)
        col = lax.broadcasted_iota(jnp.int32, (2 * ts, ts), 1)
        causal = col < jnp.bitwise_and(row, ts - 1)
        new_rows = [SbChain(lambda p=p: _nt_dot(q2[p], k_ref[0, p]), lambda w, p=p: _nn_dot(w, v_ref[0, p]),
                            _suffix_matrix(ts), causal) for p in pairs]
        run(cached_tiles(new_rows))

    @pl.when(step > 0)
    def _():
        run(cached_tiles([], [(carry_ref[p], acc_ref[p]) for p in pairs]))

    @pl.when(step == n_steps - 1)
    def _():
        for p in range(N_PAIRS):
            o_ref[0, p] = _gate(_unstack_heads(acc_ref[p], first), g_ref[0, p])


def _sb_sample_call(q, k, v, g, cache_kt, cache_vt, bs, ts):
    past = cache_kt.shape[2]
    assert past % SB_SAMPLE_KEYS == 0
    n_steps = past // SB_SAMPLE_KEYS
    blk = pl.BlockSpec((1, N_PAIRS, ts, PAIR_W), lambda b, s: (0, 0, b, 0))
    cblk = pl.BlockSpec((1, D_MODEL, SB_SAMPLE_KEYS), lambda b, s: (b, 0, n_steps - 1 - s))
    return pl.pallas_call(
        functools.partial(_sb_sample_kernel, ts=ts, n_steps=n_steps),
        grid=(bs, n_steps),
        in_specs=[blk, blk, blk, blk, cblk, cblk],
        out_specs=blk,
        out_shape=jax.ShapeDtypeStruct(q.shape, BF16),
        scratch_shapes=[pltpu.VMEM((N_PAIRS, 2 * ts, 1), F32),
                        pltpu.VMEM((N_PAIRS, 2 * ts, PAIR_W), F32)],
        name="sb_sample", compiler_params=_params(("arbitrary", "arbitrary")))(q, k, v, g, cache_kt, cache_vt)


def kernel(x_prompt, x_sample, cache_a_k, cache_a_v, cache_b_k, cache_b_v, norm_a, w_in_a, rel_bias_a,
           w_out_a, norm_kv, w_kv, norm_b, w_in_b, w_out_b, norm_f):
    b, t, d = x_prompt.shape
    bs, ts, _ = x_sample.shape
    past = cache_b_k.shape[1]
    assert d == D_MODEL and norm_a.shape[0] == 1 and norm_b.shape[0] == 1
    assert cache_a_k.shape[2] == A_CACHE_ROWS and t % SB_TQ == 0 and t >= A_CACHE_ROWS
    assert past % CHUNK == 0 and ts == CHUNK

    w_a = w_in_a[0].astype(BF16)
    wo_a = w_out_a[0].astype(BF16)
    w_kvb = w_kv.astype(BF16)
    w_b = w_in_b[0].astype(BF16)
    wo_b = w_out_b[0].astype(BF16)
    g_a = norm_a
    g_b = jnp.stack([norm_kv, norm_b[0]])
    g_f = norm_f[None]
    xs = x_sample.reshape(1, bs * ts, d)

    time_minor = lambda c: jnp.transpose(c, (0, 2, 3, 1)).reshape(c.shape[0], d, c.shape[1])

    bias_p, bias_s = _bias_call(rel_bias_a[0], past, ts)

    plan_a = (Seg(0, 0, 0, SCALE * LOG2E, 0, None), Seg(0, 0, 1, 1.0, 1, 0),
              Seg(0, 0, 2, 1.0, 2, 1), Seg(0, 0, 3, 1.0, 3, None))
    plan_b = (Seg(0, 0, 0, 1.0, 0, 0), Seg(0, 0, 1, 1.0, 1, 1),
              Seg(1, 1, 0, SCALE * LOG2E, 2, None), Seg(1, 1, 1, 1.0, 3, None))

    qp, kp, vp, gp, akp, avp = _dense_call(x_prompt, gains=g_a, weights=(w_a,), plan=plan_a,
                                           f32_last_rows=A_CACHE_ROWS, name="proj_a_prompt")
    qs, ks, vs, gs, aks, avs = _dense_call(xs, gains=g_a, weights=(w_a,), plan=plan_a,
                                           name="proj_a_sample")

    ogp = _band_prompt_call(qp, kp, vp, gp, bias_p)
    ogs = _band_sample_call(qs, ks, vs, gs, time_minor(cache_a_k[0]), time_minor(cache_a_v[0]),
                            bias_s, bs, ts)

    xp1, kbp, vbp, qbp, gbp, kbp32, vbp32 = _dense_call(
        x_prompt, og=ogp, wo=wo_a, gains=g_b, weights=(w_kvb, w_b), plan=plan_b, emit_x=True,
        name="out_a_proj_b_prompt")
    xs1, kbs, vbs, qbs, gbs, kbs32, vbs32 = _dense_call(
        xs, og=ogs, wo=wo_a, gains=g_b, weights=(w_kvb, w_b), plan=plan_b, emit_x=True,
        name="out_a_proj_b_sample")

    obp = _sb_prompt_call(qbp, kbp, vbp, gbp)
    obs = _sb_sample_call(qbs, kbs, vbs, gbs, time_minor(cache_b_k), time_minor(cache_b_v), bs, ts)

    (y_prompt,) = _dense_call(xp1, og=obp, wo=wo_b, gf=g_f, name="out_b_prompt")
    (y_sample,) = _dense_call(xs1, og=obs, wo=wo_b, gf=g_f, name="out_b_sample")

    heads = lambda a, n, rows: a.reshape(n, rows, N_HEADS, HEAD_DIM)
    return (y_prompt, y_sample.reshape(bs, ts, d),
            heads(akp, b, A_CACHE_ROWS)[None], heads(avp, b, A_CACHE_ROWS)[None],
            heads(kbp32, b, t), heads(vbp32, b, t),
            heads(aks, bs, ts)[None], heads(avs, bs, ts)[None],
            heads(kbs32, bs, ts), heads(vbs32, bs, ts))
```

```python
import functools
from typing import NamedTuple, Optional

import jax
import jax.numpy as jnp
from jax import lax
from jax.experimental import pallas as pl
from jax.experimental.pallas import tpu as pltpu

D_MODEL = 1024
N_HEADS = 16
HEAD_DIM = 64
PAIR_W = 2 * HEAD_DIM
N_PAIRS = N_HEADS // 2
CHUNK = 64
LEFT_CHUNKS = 8
A_CACHE_ROWS = LEFT_CHUNKS * CHUNK
REL_CLIP = 128
N_REL = 2 * REL_CLIP + 1
N_REL_PAD = 384
RMS_EPS = 1e-6
NEG_INF = -1e30
SCALE = HEAD_DIM ** -0.5
LOG2E = 1.4426950408889634

BAND_TQ = 128
BAND_TK = A_CACHE_ROWS + BAND_TQ
BAND_VARIANTS = A_CACHE_ROWS // BAND_TQ + 1
BAND_SAMPLE_BATCH = 2
BAND_CHAINS = 16
SB_TQ = 256
SB_TK = 256
SB_CHAINS = 4
SB_SAMPLE_KEYS = 2048
SB_Z_CLAMP = 126.0

VMEM_LIMIT = 56 * 1024 * 1024

F32 = jnp.float32
BF16 = jnp.bfloat16


def _params(sem):
    return pltpu.CompilerParams(dimension_semantics=sem, vmem_limit_bytes=VMEM_LIMIT)


class Seg(NamedTuple):
    norm: int
    w: int
    col: int
    scale: float
    pm: Optional[int]
    f32: Optional[int]


def _dense_kernel(*refs, has_resid, emit_x, n_norm, n_w, plan, final_norm, n_pm, n_f32, f32_heads):
    it = iter(refs)
    x_ref = next(it)
    og_ref = next(it) if has_resid else None
    wo_ref = next(it) if has_resid else None
    g_ref = next(it) if n_norm else None
    w_refs = [next(it) for _ in range(n_w)]
    gf_ref = next(it) if final_norm else None
    xo_ref = next(it) if emit_x else None
    pm_refs = [next(it) for _ in range(n_pm)]
    f32_refs = [next(it) for _ in range(n_f32)]
    y_ref = next(it) if final_norm else None

    x = x_ref[0]
    if has_resid:
        og = jnp.concatenate([og_ref[0, p] for p in range(N_PAIRS)], axis=1)
        x = x + jnp.dot(og, wo_ref[...], preferred_element_type=F32)
        if emit_x:
            xo_ref[0] = x
    if n_norm or final_norm:
        xn = x * lax.rsqrt(jnp.mean(x * x, axis=-1, keepdims=True) + RMS_EPS)
    if final_norm:
        y_ref[0] = xn * gf_ref[...]
    hs = [(xn * g_ref[i:i + 1, :]).astype(BF16) for i in range(n_norm)]
    for seg in plan:
        w = w_refs[seg.w][:, seg.col * D_MODEL:(seg.col + 1) * D_MODEL]
        acc = jnp.dot(hs[seg.norm], w, preferred_element_type=F32)
        if seg.f32 is not None:
            if f32_heads:
                f32_refs[seg.f32][0] = pltpu.einshape("r(hd)->rhd", acc, h=N_HEADS)
            else:
                f32_refs[seg.f32][0] = acc
        if seg.pm is not None:
            ab = (acc * seg.scale).astype(BF16) if seg.scale != 1.0 else acc.astype(BF16)
            for p in range(N_PAIRS):
                pm_refs[seg.pm][0, p] = ab[:, p * PAIR_W:(p + 1) * PAIR_W]


def _dense_call(x, *, og=None, wo=None, gains=None, weights=(), plan=(), gf=None,
                emit_x=False, f32_last_rows=None, f32_heads=False, tm=512, name="dense"):
    bx, tx, d = x.shape
    assert d == D_MODEL and tx % tm == 0
    has_resid = og is not None
    n_norm = 0 if gains is None else gains.shape[0]
    n_pm = sum(s.pm is not None for s in plan)
    n_f32 = sum(s.f32 is not None for s in plan)
    final_norm = gf is not None

    row_spec = pl.BlockSpec((1, tm, d), lambda b, t: (b, t, 0))
    pm_spec = pl.BlockSpec((1, N_PAIRS, tm, PAIR_W), lambda b, t: (b, 0, t, 0))
    whole = lambda a: pl.BlockSpec(a.shape, lambda b, t: (0,) * a.ndim, pipeline_mode=pl.Buffered(1))

    in_arrays, in_specs = [x], [row_spec]
    if has_resid:
        in_arrays += [og, wo]
        in_specs += [pm_spec, whole(wo)]
    if n_norm:
        in_arrays.append(gains)
        in_specs.append(whole(gains))
    for w in weights:
        in_arrays.append(w)
        in_specs.append(whole(w))
    if final_norm:
        in_arrays.append(gf)
        in_specs.append(whole(gf))

    out_shapes, out_specs = [], []
    if emit_x:
        out_shapes.append(jax.ShapeDtypeStruct((bx, tx, d), F32))
        out_specs.append(row_spec)
    for _ in range(n_pm):
        out_shapes.append(jax.ShapeDtypeStruct((bx, N_PAIRS, tx, PAIR_W), BF16))
        out_specs.append(pm_spec)
    for _ in range(n_f32):
        if f32_heads:
            out_shapes.append(jax.ShapeDtypeStruct((bx, tx, N_HEADS, HEAD_DIM), F32))
            out_specs.append(pl.BlockSpec((1, tm, N_HEADS, HEAD_DIM), lambda b, t: (b, t, 0, 0)))
        elif f32_last_rows is None:
            out_shapes.append(jax.ShapeDtypeStruct((bx, tx, d), F32))
            out_specs.append(row_spec)
        else:
            assert f32_last_rows == tm
            out_shapes.append(jax.ShapeDtypeStruct((bx, tm, d), F32))
            out_specs.append(pl.BlockSpec((1, tm, d), lambda b, t: (b, 0, 0)))
    if final_norm:
        out_shapes.append(jax.ShapeDtypeStruct((bx, tx, d), F32))
        out_specs.append(row_spec)

    body = functools.partial(
        _dense_kernel, has_resid=has_resid, emit_x=emit_x, n_norm=n_norm, n_w=len(weights),
        plan=tuple(plan), final_norm=final_norm, n_pm=n_pm, n_f32=n_f32, f32_heads=f32_heads)
    return pl.pallas_call(
        body, grid=(bx, tx // tm), in_specs=in_specs, out_specs=out_specs, out_shape=out_shapes,
        name=name, compiler_params=_params(("arbitrary", "arbitrary")))(*in_arrays)


def _bias_kernel(rb_ref, bp_ref, bs_ref, *, past_len, ts):
    width = 768
    d0 = A_CACHE_ROWS
    rb = rb_ref[0]
    hi = rb.astype(BF16)
    r1 = rb - hi.astype(F32)
    mid = r1.astype(BF16)
    lo = (r1 - mid.astype(F32)).astype(BF16)
    c = lax.broadcasted_iota(jnp.int32, (N_REL_PAD, width), 0)
    n = lax.broadcasted_iota(jnp.int32, (N_REL_PAD, width), 1)
    m = jnp.where(n < BAND_TK, n, n - width)
    tgt = jnp.clip(d0 - m, -REL_CLIP, REL_CLIP) + REL_CLIP
    onehot = jnp.where(c == tgt, 1.0, 0.0).astype(BF16)
    r_ext = (jnp.dot(hi, onehot, preferred_element_type=F32)
             + jnp.dot(mid, onehot, preferred_element_type=F32)
             + jnp.dot(lo, onehot, preferred_element_type=F32))

    sub = lax.broadcasted_iota(jnp.int32, (8, width), 0)
    nks = A_CACHE_ROWS + ts
    for h2 in range(2):
        base = jnp.broadcast_to(r_ext[h2:h2 + 1, :], (8, width))
        b8 = base
        for r in range(1, 8):
            b8 = jnp.where(sub == r, pltpu.roll(base, r, 1), b8)
        blocks = [b8] + [pltpu.roll(b8, 8 * gi, 1) for gi in range(1, BAND_TQ // 8)]
        toep = jnp.concatenate(blocks, axis=0)[:, :BAND_TK]

        def masked(q0, k0, rows, cols):
            qpos = q0 + lax.broadcasted_iota(jnp.int32, (rows, cols), 0)
            kpos = k0 + lax.broadcasted_iota(jnp.int32, (rows, cols), 1)
            qc0 = qpos - jnp.bitwise_and(qpos, CHUNK - 1)
            lo_k = jnp.maximum(qc0 - A_CACHE_ROWS, 0)
            ok = jnp.logical_and(kpos >= lo_k, kpos < qc0 + CHUNK)
            return jnp.where(ok, toep[:rows, :cols] * LOG2E, NEG_INF)

        for v in range(BAND_VARIANTS):
            q0 = BAND_TQ * v
            bp_ref[0, v, h2 * BAND_TQ:(h2 + 1) * BAND_TQ, :] = masked(q0, q0 - A_CACHE_ROWS, BAND_TQ, BAND_TK)
        bs_ref[0, h2 * ts:(h2 + 1) * ts, :] = masked(past_len, past_len - A_CACHE_ROWS, ts, nks)


def _bias_call(rel_bias, past_len, ts):
    rb = jnp.pad(rel_bias.reshape(N_PAIRS, 2, N_REL), ((0, 0), (0, 6), (0, N_REL_PAD - N_REL)))
    nks = A_CACHE_ROWS + ts
    return pl.pallas_call(
        functools.partial(_bias_kernel, past_len=past_len, ts=ts),
        grid=(N_PAIRS,),
        in_specs=[pl.BlockSpec((1, 8, N_REL_PAD), lambda p: (p, 0, 0))],
        out_specs=[pl.BlockSpec((1, BAND_VARIANTS, 2 * BAND_TQ, BAND_TK), lambda p: (p, 0, 0, 0)),
                   pl.BlockSpec((1, 2 * ts, nks), lambda p: (p, 0, 0))],
        out_shape=[jax.ShapeDtypeStruct((N_PAIRS, BAND_VARIANTS, 2 * BAND_TQ, BAND_TK), F32),
                   jax.ShapeDtypeStruct((N_PAIRS, 2 * ts, nks), F32)],
        name="band_bias", compiler_params=_params(("arbitrary",)))(rb)


def _first_head_lanes(rows):
    return lax.broadcasted_iota(jnp.int32, (rows, PAIR_W), 1) < HEAD_DIM


def _stack_heads(q, first):
    zero = jnp.zeros_like(q)
    return jnp.concatenate([jnp.where(first, q, zero), jnp.where(first, zero, q)], axis=0)


def _unstack_heads(o2, first):
    rows = o2.shape[0] // 2
    return jnp.where(first, o2[:rows], o2[rows:])


def _nt_dot(a, b):
    return lax.dot_general(a, b, (((1,), (1,)), ((), ())), preferred_element_type=F32)


def _gate(o, g):
    g = g.astype(F32)
    return (o * (g * jax.nn.sigmoid(g))).astype(BF16)


def _nn_dot(a, b):
    return jnp.dot(a, b, preferred_element_type=F32)


def _skewed(n_chains, n_stages, stage):
    for slot in range(n_stages + n_chains - 1):
        for c in range(n_chains):
            if 0 <= slot - c < n_stages:
                stage(c, slot - c)


def _softmax_chains(score_fns, pv_fns):
    n = len(score_fns)
    st = [dict() for _ in range(n)]

    def stage(c, s):
        d = st[c]
        if s == 0:
            d["s"] = score_fns[c]()
        elif s == 1:
            d["m"] = jnp.max(d["s"], axis=-1, keepdims=True)
        elif s == 2:
            p = jnp.exp2(d.pop("s") - d.pop("m"))
            d["l"] = jnp.sum(p, axis=-1, keepdims=True)
            d["p"] = p.astype(BF16)
        elif s == 3:
            d["o"] = pv_fns[c](d.pop("p"))
        else:
            d["o"] = d["o"] * (1.0 / d.pop("l"))

    _skewed(n, 5, stage)
    return [d["o"] for d in st]


def _band_prompt_kernel(q_ref, k_ref, v_ref, g_ref, bias_ref, o_ref, kpad, vpad, *, t):
    zeros = jnp.zeros((A_CACHE_ROWS, PAIR_W), BF16)
    kpad[0:A_CACHE_ROWS, :] = zeros
    vpad[0:A_CACHE_ROWS, :] = zeros
    kpad[A_CACHE_ROWS:, :] = k_ref[0, 0]
    vpad[A_CACHE_ROWS:, :] = v_ref[0, 0]
    first = _first_head_lanes(BAND_TQ)

    def body(i, carry):
        qts = [i * BAND_CHAINS + c for c in range(BAND_CHAINS)]
        r0s = [pl.multiple_of(qt * BAND_TQ, BAND_TQ) for qt in qts]
        def score_fn(qt, r0):
            def fn():
                q2 = _stack_heads(q_ref[0, 0, pl.ds(r0, BAND_TQ), :], first)
                kw = kpad[pl.ds(r0, BAND_TK), :]
                return _nt_dot(q2, kw) + bias_ref[0, jnp.minimum(qt, BAND_VARIANTS - 1)]
            return fn

        outs = _softmax_chains(
            [score_fn(qt, r0) for qt, r0 in zip(qts, r0s)],
            [lambda p, r0=r0: _nn_dot(p, vpad[pl.ds(r0, BAND_TK), :]) for r0 in r0s])
        for r0, o2 in zip(r0s, outs):
            o = _unstack_heads(o2, first)
            o_ref[0, 0, pl.ds(r0, BAND_TQ), :] = _gate(o, g_ref[0, 0, pl.ds(r0, BAND_TQ), :])
        return carry

    assert (t // BAND_TQ) % BAND_CHAINS == 0
    lax.fori_loop(0, t // BAND_TQ // BAND_CHAINS, body, 0)


def _band_prompt_call(q, k, v, g, bias_p):
    b, _, t, _ = q.shape
    blk = pl.BlockSpec((1, 1, t, PAIR_W), lambda p, bb: (bb, p, 0, 0))
    return pl.pallas_call(
        functools.partial(_band_prompt_kernel, t=t),
        grid=(N_PAIRS, b),
        in_specs=[blk, blk, blk, blk,
                  pl.BlockSpec((1, BAND_VARIANTS, 2 * BAND_TQ, BAND_TK), lambda p, bb: (p, 0, 0, 0))],
        out_specs=blk,
        out_shape=jax.ShapeDtypeStruct(q.shape, BF16),
        scratch_shapes=[pltpu.VMEM((t + A_CACHE_ROWS, PAIR_W), BF16),
                        pltpu.VMEM((t + A_CACHE_ROWS, PAIR_W), BF16)],
        name="band_prompt", compiler_params=_params(("arbitrary", "arbitrary")))(q, k, v, g, bias_p)


def _band_sample_kernel(q_ref, k_ref, v_ref, g_ref, ckt_ref, cvt_ref, bias_ref, o_ref, *, ts):
    first = _first_head_lanes(ts)
    units = [(i, p) for i in range(BAND_SAMPLE_BATCH) for p in range(N_PAIRS)]
    rows = lambda p: slice(p * PAIR_W, (p + 1) * PAIR_W)
    new = lambda i: slice(i * ts, (i + 1) * ts)

    def score_fn(i, p):
        def fn():
            q2 = _stack_heads(q_ref[0, p, new(i)], first)
            s_old = _nn_dot(q2, ckt_ref[i, rows(p), :].astype(BF16))
            s_new = _nt_dot(q2, k_ref[0, p, new(i)])
            return jnp.concatenate([s_old, s_new], axis=1) + bias_ref[p]
        return fn

    def pv_fn(i, p):
        def fn(prob):
            return (_nt_dot(prob[:, :A_CACHE_ROWS], cvt_ref[i, rows(p), :].astype(BF16))
                    + _nn_dot(prob[:, A_CACHE_ROWS:], v_ref[0, p, new(i)]))
        return fn

    outs = _softmax_chains([score_fn(i, p) for i, p in units], [pv_fn(i, p) for i, p in units])
    for (i, p), o2 in zip(units, outs):
        o_ref[0, p, new(i)] = _gate(_unstack_heads(o2, first), g_ref[0, p, new(i)])


def _band_sample_call(q, k, v, g, cache_kt, cache_vt, bias_s, bs, ts):
    nb = BAND_SAMPLE_BATCH
    assert bs % nb == 0
    blk = pl.BlockSpec((1, N_PAIRS, nb * ts, PAIR_W), lambda b: (0, 0, b, 0))
    cblk = pl.BlockSpec((nb, D_MODEL, A_CACHE_ROWS), lambda b: (b, 0, 0))
    return pl.pallas_call(
        functools.partial(_band_sample_kernel, ts=ts),
        grid=(bs // nb,),
        in_specs=[blk, blk, blk, blk, cblk, cblk,
                  pl.BlockSpec(bias_s.shape, lambda b: (0, 0, 0))],
        out_specs=blk,
        out_shape=jax.ShapeDtypeStruct(q.shape, BF16),
        name="band_sample", compiler_params=_params(("arbitrary",)))(q, k, v, g, cache_kt, cache_vt, bias_s)


def _suffix_matrix(n):
    r = lax.broadcasted_iota(jnp.int32, (n, n), 0)
    c = lax.broadcasted_iota(jnp.int32, (n, n), 1)
    return jnp.where(r > c, 1.0, 0.0).astype(BF16)


class SbChain(NamedTuple):
    score_fn: object
    pv_fn: object
    suffix: jax.Array
    causal: Optional[jax.Array] = None
    prev: Optional[int] = None
    carry: Optional[jax.Array] = None
    acc: Optional[jax.Array] = None


def _sb_tiles(chains):
    st = [dict() for _ in chains]

    def stage(c, s):
        ch, d = chains[c], st[c]
        if s == 0:
            d["z"] = ch.score_fn()
        elif s == 1:
            z = d.pop("z")
            sp = jnp.maximum(jnp.log2(1.0 + jnp.exp2(jnp.minimum(z, SB_Z_CLAMP))), z)
            if ch.causal is not None:
                sp = jnp.where(ch.causal, sp, 0.0)
            total = jnp.sum(sp, axis=-1, keepdims=True)
            carry = ch.carry if ch.prev is None else st[ch.prev]["carry"]
            d["sp"] = sp.astype(BF16)
            d["zs"] = z - sp if carry is None else (z - sp) - carry
            d["carry"] = total if carry is None else carry + total
        elif s == 2:
            d["later"] = _nn_dot(d.pop("sp"), ch.suffix)
        elif s == 3:
            w = jnp.exp2(d.pop("zs") - d.pop("later"))
            if ch.causal is not None:
                w = jnp.where(ch.causal, w, 0.0)
            d["w"] = w.astype(BF16)
        else:
            pv = ch.pv_fn(d.pop("w"))
            acc = ch.acc if ch.prev is None else st[ch.prev]["acc"]
            d["acc"] = pv if acc is None else acc + pv

    _skewed(len(chains), 5, stage)
    return [(d["carry"], d["acc"]) for d in st]


def _sb_prompt_kernel(q_ref, k_ref, v_ref, g_ref, o_ref, *, t):
    first = _first_head_lanes(SB_TQ)
    suffix = _suffix_matrix(SB_TK)
    row = lax.broadcasted_iota(jnp.int32, (SB_TQ, SB_TK), 0)
    col = lax.broadcasted_iota(jnp.int32, (SB_TQ, SB_TK), 1)
    causal = col < row

    pairs = range(SB_CHAINS)
    chain_pair = [p for p in pairs for _ in range(2)]

    n = len(chain_pair)

    def q_body(qt, c0):
        r0 = pl.multiple_of(qt * SB_TQ, SB_TQ)
        qh = []
        for p in pairs:
            q = q_ref[0, p, pl.ds(r0, SB_TQ), :]
            zero = jnp.zeros_like(q)
            qh += [jnp.where(first, q, zero), jnp.where(first, zero, q)]

        def tile(k0, mask=None, state=None, prev0=None):
            return [SbChain(lambda c=c, p=p: _nt_dot(qh[c], k_ref[0, p, pl.ds(k0, SB_TK), :]),
                            lambda w, p=p: _nn_dot(w, v_ref[0, p, pl.ds(k0, SB_TK), :]),
                            suffix, mask, None if prev0 is None else prev0 + c,
                            None if state is None else state[2 * c],
                            None if state is None else state[2 * c + 1])
                    for c, p in enumerate(chain_pair)]

        def flat(results):
            return tuple(x for r in results for x in r)

        def own_tile_alone():
            return flat(_sb_tiles(tile(r0, causal)))

        def own_tile_and_next():
            k1 = pl.multiple_of(r0 - SB_TK, SB_TK)
            return flat(_sb_tiles(tile(r0, causal) + tile(k1, prev0=0))[n:])

        odd = jnp.bitwise_and(qt, 1)
        state = lax.cond(odd == 1, own_tile_and_next, own_tile_alone)

        def k_body(i, st):
            ka = pl.multiple_of((qt - 1 - odd - 2 * i) * SB_TK, SB_TK)
            kb = pl.multiple_of(ka - SB_TK, SB_TK)
            return flat(_sb_tiles(tile(ka, state=st) + tile(kb, prev0=0))[n:])

        state = lax.fori_loop(0, lax.shift_right_logical(qt, 1), k_body, state)
        for p in pairs:
            o = jnp.where(first, state[4 * p + 1], state[4 * p + 3])
            o_ref[0, p, pl.ds(r0, SB_TQ), :] = _gate(o, g_ref[0, p, pl.ds(r0, SB_TQ), :])
        return c0

    lax.fori_loop(0, t // SB_TQ, q_body, 0)


def _sb_prompt_call(q, k, v, g):
    b, _, t, _ = q.shape
    blk = pl.BlockSpec((1, SB_CHAINS, t, PAIR_W), lambda bb, p: (bb, p, 0, 0))
    return pl.pallas_call(
        functools.partial(_sb_prompt_kernel, t=t),
        grid=(b, N_PAIRS // SB_CHAINS),
        in_specs=[blk, blk, blk, blk],
        out_specs=blk,
        out_shape=jax.ShapeDtypeStruct(q.shape, BF16),
        name="sb_prompt", compiler_params=_params(("arbitrary", "arbitrary")))(q, k, v, g)


def _sb_sample_kernel(q_ref, k_ref, v_ref, g_ref, ckt_ref, cvt_ref, o_ref, carry_ref, acc_ref, *, ts, n_steps):
    step = pl.program_id(1)
    first = _first_head_lanes(ts)
    pairs = range(N_PAIRS)
    rows = [slice(p * PAIR_W, (p + 1) * PAIR_W) for p in pairs]
    q2 = [_stack_heads(q_ref[0, p], first) for p in pairs]
    suffix = _suffix_matrix(SB_TK)

    def cached_tiles(chains, state=None):
        for i in reversed(range(SB_SAMPLE_KEYS // SB_TK)):
            keys = slice(i * SB_TK, (i + 1) * SB_TK)
            base = len(chains) - N_PAIRS
            chains = chains + [
                SbChain(lambda p=p, keys=keys: _nn_dot(q2[p], ckt_ref[0, rows[p], keys].astype(BF16)),
                        lambda w, p=p, keys=keys: _nt_dot(w, cvt_ref[0, rows[p], keys].astype(BF16)),
                        suffix, None, base + p if base >= 0 else None,
                        None if base >= 0 else state[p][0], None if base >= 0 else state[p][1])
                for p in pairs]
        return chains

    def run(chains):
        out = _sb_tiles(chains)[-N_PAIRS:]
        for p in pairs:
            carry_ref[p], acc_ref[p] = out[p]

    @pl.when(step == 0)
    def _():
        row = lax.broadcasted_iota(jnp.int32, (2 * ts, ts), 0)
        col = lax.broadcasted_iota(jnp.int32, (2 * ts, ts), 1)
        causal = col < jnp.bitwise_and(row, ts - 1)
        new_rows = [SbChain(lambda p=p: _nt_dot(q2[p], k_ref[0, p]), lambda w, p=p: _nn_dot(w, v_ref[0, p]),
                            _suffix_matrix(ts), causal) for p in pairs]
        run(cached_tiles(new_rows))

    @pl.when(step > 0)
    def _():
        run(cached_tiles([], [(carry_ref[p], acc_ref[p]) for p in pairs]))

    @pl.when(step == n_steps - 1)
    def _():
        for p in range(N_PAIRS):
            o_ref[0, p] = _gate(_unstack_heads(acc_ref[p], first), g_ref[0, p])


def _sb_sample_call(q, k, v, g, cache_kt, cache_vt, bs, ts):
    past = cache_kt.shape[2]
    assert past % SB_SAMPLE_KEYS == 0
    n_steps = past // SB_SAMPLE_KEYS
    blk = pl.BlockSpec((1, N_PAIRS, ts, PAIR_W), lambda b, s: (0, 0, b, 0))
    cblk = pl.BlockSpec((1, D_MODEL, SB_SAMPLE_KEYS), lambda b, s: (b, 0, n_steps - 1 - s))
    return pl.pallas_call(
        functools.partial(_sb_sample_kernel, ts=ts, n_steps=n_steps),
        grid=(bs, n_steps),
        in_specs=[blk, blk, blk, blk, cblk, cblk],
        out_specs=blk,
        out_shape=jax.ShapeDtypeStruct(q.shape, BF16),
        scratch_shapes=[pltpu.VMEM((N_PAIRS, 2 * ts, 1), F32),
                        pltpu.VMEM((N_PAIRS, 2 * ts, PAIR_W), F32)],
        name="sb_sample", compiler_params=_params(("arbitrary", "arbitrary")))(q, k, v, g, cache_kt, cache_vt)


def kernel(x_prompt, x_sample, cache_a_k, cache_a_v, cache_b_k, cache_b_v, norm_a, w_in_a, rel_bias_a,
           w_out_a, norm_kv, w_kv, norm_b, w_in_b, w_out_b, norm_f):
    b, t, d = x_prompt.shape
    bs, ts, _ = x_sample.shape
    past = cache_b_k.shape[1]
    assert d == D_MODEL and norm_a.shape[0] == 1 and norm_b.shape[0] == 1
    assert cache_a_k.shape[2] == A_CACHE_ROWS and t % SB_TQ == 0 and t >= A_CACHE_ROWS
    assert past % CHUNK == 0 and ts == CHUNK

    w_a = w_in_a[0].astype(BF16)
    wo_a = w_out_a[0].astype(BF16)
    w_kvb = w_kv.astype(BF16)
    w_b = w_in_b[0].astype(BF16)
    wo_b = w_out_b[0].astype(BF16)
    g_a = norm_a
    g_b = jnp.stack([norm_kv, norm_b[0]])
    g_f = norm_f[None]
    xs = x_sample.reshape(1, bs * ts, d)

    time_minor = lambda c: jnp.transpose(c, (0, 2, 3, 1)).reshape(c.shape[0], d, c.shape[1])

    bias_p, bias_s = _bias_call(rel_bias_a[0], past, ts)

    plan_a = (Seg(0, 0, 0, SCALE * LOG2E, 0, None), Seg(0, 0, 1, 1.0, 1, 0),
              Seg(0, 0, 2, 1.0, 2, 1), Seg(0, 0, 3, 1.0, 3, None))
    plan_b = (Seg(0, 0, 0, 1.0, 0, 0), Seg(0, 0, 1, 1.0, 1, 1),
              Seg(1, 1, 0, SCALE * LOG2E, 2, None), Seg(1, 1, 1, 1.0, 3, None))

    qp, kp, vp, gp, akp, avp = _dense_call(x_prompt, gains=g_a, weights=(w_a,), plan=plan_a,
                                           f32_last_rows=A_CACHE_ROWS, name="proj_a_prompt")
    qs, ks, vs, gs, aks, avs = _dense_call(xs, gains=g_a, weights=(w_a,), plan=plan_a, f32_heads=True,
                                           name="proj_a_sample")

    ogp = _band_prompt_call(qp, kp, vp, gp, bias_p)
    ogs = _band_sample_call(qs, ks, vs, gs, time_minor(cache_a_k[0]), time_minor(cache_a_v[0]),
                            bias_s, bs, ts)

    xp1, kbp, vbp, qbp, gbp, kbp32, vbp32 = _dense_call(
        x_prompt, og=ogp, wo=wo_a, gains=g_b, weights=(w_kvb, w_b), plan=plan_b, emit_x=True,
        name="out_a_proj_b_prompt")
    xs1, kbs, vbs, qbs, gbs, kbs32, vbs32 = _dense_call(
        xs, og=ogs, wo=wo_a, gains=g_b, weights=(w_kvb, w_b), plan=plan_b, emit_x=True, f32_heads=True,
        name="out_a_proj_b_sample")

    obp = _sb_prompt_call(qbp, kbp, vbp, gbp)
    obs = _sb_sample_call(qbs, kbs, vbs, gbs, time_minor(cache_b_k), time_minor(cache_b_v), bs, ts)

    (y_prompt,) = _dense_call(xp1, og=obp, wo=wo_b, gf=g_f, name="out_b_prompt")
    (y_sample,) = _dense_call(xs1, og=obs, wo=wo_b, gf=g_f, name="out_b_sample")

    heads = lambda a, n, rows: a.reshape(n, rows, N_HEADS, HEAD_DIM)
    return (y_prompt, y_sample.reshape(bs, ts, d),
            heads(akp, b, A_CACHE_ROWS)[None], heads(avp, b, A_CACHE_ROWS)[None],
            heads(kbp32, b, t), heads(vbp32, b, t),
            heads(aks, bs, ts)[None], heads(avs, bs, ts)[None],
            heads(kbs32, bs, ts), heads(vbs32, bs, ts))
```

```python
import functools
from typing import NamedTuple, Optional

import jax
import jax.numpy as jnp
from jax import lax
from jax.experimental import pallas as pl
from jax.experimental.pallas import tpu as pltpu

D_MODEL = 1024
N_HEADS = 16
HEAD_DIM = 64
PAIR_W = 2 * HEAD_DIM
N_PAIRS = N_HEADS // 2
CHUNK = 64
LEFT_CHUNKS = 8
A_CACHE_ROWS = LEFT_CHUNKS * CHUNK
REL_CLIP = 128
N_REL = 2 * REL_CLIP + 1
N_REL_PAD = 384
RMS_EPS = 1e-6
NEG_INF = -1e30
SCALE = HEAD_DIM ** -0.5
LOG2E = 1.4426950408889634

BAND_TQ = 128
BAND_TK = A_CACHE_ROWS + BAND_TQ
BAND_VARIANTS = A_CACHE_ROWS // BAND_TQ + 1
BAND_SAMPLE_BATCH = 2
BAND_CHAINS = 16
SB_TQ = 256
SB_TK = 256
SB_CHAINS = 4
SB_SAMPLE_KEYS = 2048
SB_Z_CLAMP = 126.0

VMEM_LIMIT = 56 * 1024 * 1024

F32 = jnp.float32
BF16 = jnp.bfloat16


def _params(sem):
    return pltpu.CompilerParams(dimension_semantics=sem, vmem_limit_bytes=VMEM_LIMIT)


class Seg(NamedTuple):
    norm: int
    w: int
    col: int
    scale: float
    pm: Optional[int]
    f32: Optional[int]


def _dense_kernel(*refs, has_resid, emit_x, n_norm, n_w, plan, final_norm, n_pm, n_f32, f32_heads):
    it = iter(refs)
    x_ref = next(it)
    og_ref = next(it) if has_resid else None
    wo_ref = next(it) if has_resid else None
    g_ref = next(it) if n_norm else None
    w_refs = [next(it) for _ in range(n_w)]
    gf_ref = next(it) if final_norm else None
    xo_ref = next(it) if emit_x else None
    pm_refs = [next(it) for _ in range(n_pm)]
    f32_refs = [next(it) for _ in range(n_f32)]
    y_ref = next(it) if final_norm else None

    x = x_ref[0]
    if has_resid:
        og = jnp.concatenate([og_ref[0, p] for p in range(N_PAIRS)], axis=1)
        x = x + jnp.dot(og, wo_ref[...], preferred_element_type=F32)
        if emit_x:
            xo_ref[0] = x
    if n_norm or final_norm:
        xn = x * lax.rsqrt(jnp.mean(x * x, axis=-1, keepdims=True) + RMS_EPS)
    if final_norm:
        y_ref[0] = xn * gf_ref[...]
    hs = [(xn * g_ref[i:i + 1, :]).astype(BF16) for i in range(n_norm)]
    for seg in plan:
        w = w_refs[seg.w][:, seg.col * D_MODEL:(seg.col + 1) * D_MODEL]
        acc = jnp.dot(hs[seg.norm], w, preferred_element_type=F32)
        if seg.f32 is not None:
            if f32_heads:
                f32_refs[seg.f32][0] = pltpu.einshape("r(hd)->rhd", acc, h=N_HEADS)
            else:
                f32_refs[seg.f32][0] = acc
        if seg.pm is not None:
            ab = (acc * seg.scale).astype(BF16) if seg.scale != 1.0 else acc.astype(BF16)
            for p in range(N_PAIRS):
                pm_refs[seg.pm][0, p] = ab[:, p * PAIR_W:(p + 1) * PAIR_W]


def _dense_call(x, *, og=None, wo=None, gains=None, weights=(), plan=(), gf=None,
                emit_x=False, f32_last_rows=None, f32_heads=False, tm=512, name="dense"):
    bx, tx, d = x.shape
    assert d == D_MODEL and tx % tm == 0
    has_resid = og is not None
    n_norm = 0 if gains is None else gains.shape[0]
    n_pm = sum(s.pm is not None for s in plan)
    n_f32 = sum(s.f32 is not None for s in plan)
    final_norm = gf is not None

    row_spec = pl.BlockSpec((1, tm, d), lambda b, t: (b, t, 0))
    pm_spec = pl.BlockSpec((1, N_PAIRS, tm, PAIR_W), lambda b, t: (b, 0, t, 0))
    whole = lambda a: pl.BlockSpec(a.shape, lambda b, t: (0,) * a.ndim, pipeline_mode=pl.Buffered(1))

    in_arrays, in_specs = [x], [row_spec]
    if has_resid:
        in_arrays += [og, wo]
        in_specs += [pm_spec, whole(wo)]
    if n_norm:
        in_arrays.append(gains)
        in_specs.append(whole(gains))
    for w in weights:
        in_arrays.append(w)
        in_specs.append(whole(w))
    if final_norm:
        in_arrays.append(gf)
        in_specs.append(whole(gf))

    out_shapes, out_specs = [], []
    if emit_x:
        out_shapes.append(jax.ShapeDtypeStruct((bx, tx, d), F32))
        out_specs.append(row_spec)
    for _ in range(n_pm):
        out_shapes.append(jax.ShapeDtypeStruct((bx, N_PAIRS, tx, PAIR_W), BF16))
        out_specs.append(pm_spec)
    for _ in range(n_f32):
        if f32_heads:
            out_shapes.append(jax.ShapeDtypeStruct((bx, tx, N_HEADS, HEAD_DIM), F32))
            out_specs.append(pl.BlockSpec((1, tm, N_HEADS, HEAD_DIM), lambda b, t: (b, t, 0, 0)))
        elif f32_last_rows is None:
            out_shapes.append(jax.ShapeDtypeStruct((bx, tx, d), F32))
            out_specs.append(row_spec)
        else:
            assert f32_last_rows == tm
            out_shapes.append(jax.ShapeDtypeStruct((bx, tm, d), F32))
            out_specs.append(pl.BlockSpec((1, tm, d), lambda b, t: (b, 0, 0)))
    if final_norm:
        out_shapes.append(jax.ShapeDtypeStruct((bx, tx, d), F32))
        out_specs.append(row_spec)

    body = functools.partial(
        _dense_kernel, has_resid=has_resid, emit_x=emit_x, n_norm=n_norm, n_w=len(weights),
        plan=tuple(plan), final_norm=final_norm, n_pm=n_pm, n_f32=n_f32, f32_heads=f32_heads)
    return pl.pallas_call(
        body, grid=(bx, tx // tm), in_specs=in_specs, out_specs=out_specs, out_shape=out_shapes,
        name=name, compiler_params=_params(("arbitrary", "arbitrary")))(*in_arrays)


def _bias_kernel(rb_ref, bp_ref, bs_ref, *, past_len, ts):
    width = 768
    d0 = A_CACHE_ROWS
    rb = rb_ref[0]
    hi = rb.astype(BF16)
    r1 = rb - hi.astype(F32)
    mid = r1.astype(BF16)
    lo = (r1 - mid.astype(F32)).astype(BF16)
    c = lax.broadcasted_iota(jnp.int32, (N_REL_PAD, width), 0)
    n = lax.broadcasted_iota(jnp.int32, (N_REL_PAD, width), 1)
    m = jnp.where(n < BAND_TK, n, n - width)
    tgt = jnp.clip(d0 - m, -REL_CLIP, REL_CLIP) + REL_CLIP
    onehot = jnp.where(c == tgt, 1.0, 0.0).astype(BF16)
    r_ext = (jnp.dot(hi, onehot, preferred_element_type=F32)
             + jnp.dot(mid, onehot, preferred_element_type=F32)
             + jnp.dot(lo, onehot, preferred_element_type=F32))

    sub = lax.broadcasted_iota(jnp.int32, (8, width), 0)
    nks = A_CACHE_ROWS + ts
    for h2 in range(2):
        base = jnp.broadcast_to(r_ext[h2:h2 + 1, :], (8, width))
        b8 = base
        for r in range(1, 8):
            b8 = jnp.where(sub == r, pltpu.roll(base, r, 1), b8)
        blocks = [b8] + [pltpu.roll(b8, 8 * gi, 1) for gi in range(1, BAND_TQ // 8)]
        toep = jnp.concatenate(blocks, axis=0)[:, :BAND_TK]

        def masked(q0, k0, rows, cols):
            qpos = q0 + lax.broadcasted_iota(jnp.int32, (rows, cols), 0)
            kpos = k0 + lax.broadcasted_iota(jnp.int32, (rows, cols), 1)
            qc0 = qpos - jnp.bitwise_and(qpos, CHUNK - 1)
            lo_k = jnp.maximum(qc0 - A_CACHE_ROWS, 0)
            ok = jnp.logical_and(kpos >= lo_k, kpos < qc0 + CHUNK)
            return jnp.where(ok, toep[:rows, :cols] * LOG2E, NEG_INF)

        for v in range(BAND_VARIANTS):
            q0 = BAND_TQ * v
            bp_ref[0, v, h2 * BAND_TQ:(h2 + 1) * BAND_TQ, :] = masked(q0, q0 - A_CACHE_ROWS, BAND_TQ, BAND_TK)
        bs_ref[0, h2 * ts:(h2 + 1) * ts, :] = masked(past_len, past_len - A_CACHE_ROWS, ts, nks)


def _bias_call(rel_bias, past_len, ts):
    rb = jnp.pad(rel_bias.reshape(N_PAIRS, 2, N_REL), ((0, 0), (0, 6), (0, N_REL_PAD - N_REL)))
    nks = A_CACHE_ROWS + ts
    return pl.pallas_call(
        functools.partial(_bias_kernel, past_len=past_len, ts=ts),
        grid=(N_PAIRS,),
        in_specs=[pl.BlockSpec((1, 8, N_REL_PAD), lambda p: (p, 0, 0))],
        out_specs=[pl.BlockSpec((1, BAND_VARIANTS, 2 * BAND_TQ, BAND_TK), lambda p: (p, 0, 0, 0)),
                   pl.BlockSpec((1, 2 * ts, nks), lambda p: (p, 0, 0))],
        out_shape=[jax.ShapeDtypeStruct((N_PAIRS, BAND_VARIANTS, 2 * BAND_TQ, BAND_TK), F32),
                   jax.ShapeDtypeStruct((N_PAIRS, 2 * ts, nks), F32)],
        name="band_bias", compiler_params=_params(("arbitrary",)))(rb)


def _first_head_lanes(rows):
    return lax.broadcasted_iota(jnp.int32, (rows, PAIR_W), 1) < HEAD_DIM


def _stack_heads(q, first):
    zero = jnp.zeros_like(q)
    return jnp.concatenate([jnp.where(first, q, zero), jnp.where(first, zero, q)], axis=0)


def _unstack_heads(o2, first):
    rows = o2.shape[0] // 2
    return jnp.where(first, o2[:rows], o2[rows:])


def _nt_dot(a, b):
    return lax.dot_general(a, b, (((1,), (1,)), ((), ())), preferred_element_type=F32)


def _gate(o, g):
    g = g.astype(F32)
    return (o * (g * jax.nn.sigmoid(g))).astype(BF16)


def _nn_dot(a, b):
    return jnp.dot(a, b, preferred_element_type=F32)


def _skewed(n_chains, n_stages, stage):
    for slot in range(n_stages + n_chains - 1):
        for c in range(n_chains):
            if 0 <= slot - c < n_stages:
                stage(c, slot - c)


def _softmax_chains(score_fns, pv_fns):
    n = len(score_fns)
    st = [dict() for _ in range(n)]

    def stage(c, s):
        d = st[c]
        if s == 0:
            d["s"] = score_fns[c]()
        elif s == 1:
            d["m"] = jnp.max(d["s"], axis=-1, keepdims=True)
        elif s == 2:
            p = jnp.exp2(d.pop("s") - d.pop("m"))
            d["l"] = jnp.sum(p, axis=-1, keepdims=True)
            d["p"] = p.astype(BF16)
        elif s == 3:
            d["o"] = pv_fns[c](d.pop("p"))
        else:
            d["o"] = d["o"] * (1.0 / d.pop("l"))

    _skewed(n, 5, stage)
    return [d["o"] for d in st]


def _band_prompt_kernel(q_ref, k_ref, v_ref, g_ref, bias_ref, o_ref, kpad, vpad, *, t):
    zeros = jnp.zeros((A_CACHE_ROWS, PAIR_W), BF16)
    kpad[0:A_CACHE_ROWS, :] = zeros
    vpad[0:A_CACHE_ROWS, :] = zeros
    kpad[A_CACHE_ROWS:, :] = k_ref[0, 0]
    vpad[A_CACHE_ROWS:, :] = v_ref[0, 0]
    first = _first_head_lanes(BAND_TQ)

    def body(i, carry):
        qts = [i * BAND_CHAINS + c for c in range(BAND_CHAINS)]
        r0s = [pl.multiple_of(qt * BAND_TQ, BAND_TQ) for qt in qts]
        def score_fn(qt, r0):
            def fn():
                q2 = _stack_heads(q_ref[0, 0, pl.ds(r0, BAND_TQ), :], first)
                kw = kpad[pl.ds(r0, BAND_TK), :]
                return _nt_dot(q2, kw) + bias_ref[0, jnp.minimum(qt, BAND_VARIANTS - 1)]
            return fn

        outs = _softmax_chains(
            [score_fn(qt, r0) for qt, r0 in zip(qts, r0s)],
            [lambda p, r0=r0: _nn_dot(p, vpad[pl.ds(r0, BAND_TK), :]) for r0 in r0s])
        for r0, o2 in zip(r0s, outs):
            o = _unstack_heads(o2, first)
            o_ref[0, 0, pl.ds(r0, BAND_TQ), :] = _gate(o, g_ref[0, 0, pl.ds(r0, BAND_TQ), :])
        return carry

    assert (t // BAND_TQ) % BAND_CHAINS == 0
    lax.fori_loop(0, t // BAND_TQ // BAND_CHAINS, body, 0)


def _band_prompt_call(q, k, v, g, bias_p):
    b, _, t, _ = q.shape
    blk = pl.BlockSpec((1, 1, t, PAIR_W), lambda p, bb: (bb, p, 0, 0))
    return pl.pallas_call(
        functools.partial(_band_prompt_kernel, t=t),
        grid=(N_PAIRS, b),
        in_specs=[blk, blk, blk, blk,
                  pl.BlockSpec((1, BAND_VARIANTS, 2 * BAND_TQ, BAND_TK), lambda p, bb: (p, 0, 0, 0))],
        out_specs=blk,
        out_shape=jax.ShapeDtypeStruct(q.shape, BF16),
        scratch_shapes=[pltpu.VMEM((t + A_CACHE_ROWS, PAIR_W), BF16),
                        pltpu.VMEM((t + A_CACHE_ROWS, PAIR_W), BF16)],
        name="band_prompt", compiler_params=_params(("arbitrary", "arbitrary")))(q, k, v, g, bias_p)


def _band_sample_kernel(q_ref, k_ref, v_ref, g_ref, *refs, ts):
    ckt_refs, cvt_refs = refs[:N_PAIRS], refs[N_PAIRS:2 * N_PAIRS]
    bias_ref, o_ref = refs[2 * N_PAIRS:]
    first = _first_head_lanes(ts)
    units = [(i, p) for i in range(BAND_SAMPLE_BATCH) for p in range(N_PAIRS)]
    new = lambda i: slice(i * ts, (i + 1) * ts)

    def score_fn(i, p):
        def fn():
            q2 = _stack_heads(q_ref[0, p, new(i)], first)
            s_old = _nn_dot(q2, ckt_refs[p][i].astype(BF16))
            s_new = _nt_dot(q2, k_ref[0, p, new(i)])
            return jnp.concatenate([s_old, s_new], axis=1) + bias_ref[p]
        return fn

    def pv_fn(i, p):
        def fn(prob):
            return (_nt_dot(prob[:, :A_CACHE_ROWS], cvt_refs[p][i].astype(BF16))
                    + _nn_dot(prob[:, A_CACHE_ROWS:], v_ref[0, p, new(i)]))
        return fn

    outs = _softmax_chains([score_fn(i, p) for i, p in units], [pv_fn(i, p) for i, p in units])
    for (i, p), o2 in zip(units, outs):
        o_ref[0, p, new(i)] = _gate(_unstack_heads(o2, first), g_ref[0, p, new(i)])


def _band_sample_call(q, k, v, g, cache_kt, cache_vt, bias_s, bs, ts):
    nb = BAND_SAMPLE_BATCH
    assert bs % nb == 0
    blk = pl.BlockSpec((1, N_PAIRS, nb * ts, PAIR_W), lambda b: (0, 0, b, 0))
    cblks = [pl.BlockSpec((nb, PAIR_W, A_CACHE_ROWS), lambda b, p=p: (b, p, 0)) for p in range(N_PAIRS)]
    return pl.pallas_call(
        functools.partial(_band_sample_kernel, ts=ts),
        grid=(bs // nb,),
        in_specs=[blk, blk, blk, blk] + cblks + cblks + [pl.BlockSpec(bias_s.shape, lambda b: (0, 0, 0))],
        out_specs=blk,
        out_shape=jax.ShapeDtypeStruct(q.shape, BF16),
        name="band_sample", compiler_params=_params(("arbitrary",)))(
            q, k, v, g, *([cache_kt] * N_PAIRS), *([cache_vt] * N_PAIRS), bias_s)


def _suffix_matrix(n):
    r = lax.broadcasted_iota(jnp.int32, (n, n), 0)
    c = lax.broadcasted_iota(jnp.int32, (n, n), 1)
    return jnp.where(r > c, 1.0, 0.0).astype(BF16)


class SbChain(NamedTuple):
    score_fn: object
    pv_fn: object
    suffix: jax.Array
    causal: Optional[jax.Array] = None
    prev: Optional[int] = None
    carry: Optional[jax.Array] = None
    acc: Optional[jax.Array] = None


def _sb_tiles(chains):
    st = [dict() for _ in chains]

    def stage(c, s):
        ch, d = chains[c], st[c]
        if s == 0:
            d["z"] = ch.score_fn()
        elif s == 1:
            z = d.pop("z")
            sp = jnp.maximum(jnp.log2(1.0 + jnp.exp2(jnp.minimum(z, SB_Z_CLAMP))), z)
            if ch.causal is not None:
                sp = jnp.where(ch.causal, sp, 0.0)
            total = jnp.sum(sp, axis=-1, keepdims=True)
            carry = ch.carry if ch.prev is None else st[ch.prev]["carry"]
            d["sp"] = sp.astype(BF16)
            d["zs"] = z - sp if carry is None else (z - sp) - carry
            d["carry"] = total if carry is None else carry + total
        elif s == 2:
            d["later"] = _nn_dot(d.pop("sp"), ch.suffix)
        elif s == 3:
            w = jnp.exp2(d.pop("zs") - d.pop("later"))
            if ch.causal is not None:
                w = jnp.where(ch.causal, w, 0.0)
            d["w"] = w.astype(BF16)
        else:
            pv = ch.pv_fn(d.pop("w"))
            acc = ch.acc if ch.prev is None else st[ch.prev]["acc"]
            d["acc"] = pv if acc is None else acc + pv

    _skewed(len(chains), 5, stage)
    return [(d["carry"], d["acc"]) for d in st]


def _sb_prompt_kernel(q_ref, k_ref, v_ref, g_ref, o_ref, *, t):
    first = _first_head_lanes(SB_TQ)
    suffix = _suffix_matrix(SB_TK)
    row = lax.broadcasted_iota(jnp.int32, (SB_TQ, SB_TK), 0)
    col = lax.broadcasted_iota(jnp.int32, (SB_TQ, SB_TK), 1)
    causal = col < row

    pairs = range(SB_CHAINS)
    chain_pair = [p for p in pairs for _ in range(2)]

    n = len(chain_pair)

    def q_body(qt, c0):
        r0 = pl.multiple_of(qt * SB_TQ, SB_TQ)
        qh = []
        for p in pairs:
            q = q_ref[0, p, pl.ds(r0, SB_TQ), :]
            zero = jnp.zeros_like(q)
            qh += [jnp.where(first, q, zero), jnp.where(first, zero, q)]

        def tile(k0, mask=None, state=None, prev0=None):
            return [SbChain(lambda c=c, p=p: _nt_dot(qh[c], k_ref[0, p, pl.ds(k0, SB_TK), :]),
                            lambda w, p=p: _nn_dot(w, v_ref[0, p, pl.ds(k0, SB_TK), :]),
                            suffix, mask, None if prev0 is None else prev0 + c,
                            None if state is None else state[2 * c],
                            None if state is None else state[2 * c + 1])
                    for c, p in enumerate(chain_pair)]

        def flat(results):
            return tuple(x for r in results for x in r)

        def own_tile_alone():
            return flat(_sb_tiles(tile(r0, causal)))

        def own_tile_and_next():
            k1 = pl.multiple_of(r0 - SB_TK, SB_TK)
            return flat(_sb_tiles(tile(r0, causal) + tile(k1, prev0=0))[n:])

        odd = jnp.bitwise_and(qt, 1)
        state = lax.cond(odd == 1, own_tile_and_next, own_tile_alone)

        def k_body(i, st):
            ka = pl.multiple_of((qt - 1 - odd - 2 * i) * SB_TK, SB_TK)
            kb = pl.multiple_of(ka - SB_TK, SB_TK)
            return flat(_sb_tiles(tile(ka, state=st) + tile(kb, prev0=0))[n:])

        state = lax.fori_loop(0, lax.shift_right_logical(qt, 1), k_body, state)
        for p in pairs:
            o = jnp.where(first, state[4 * p + 1], state[4 * p + 3])
            o_ref[0, p, pl.ds(r0, SB_TQ), :] = _gate(o, g_ref[0, p, pl.ds(r0, SB_TQ), :])
        return c0

    lax.fori_loop(0, t // SB_TQ, q_body, 0)


def _sb_prompt_call(q, k, v, g):
    b, _, t, _ = q.shape
    blk = pl.BlockSpec((1, SB_CHAINS, t, PAIR_W), lambda bb, p: (bb, p, 0, 0))
    return pl.pallas_call(
        functools.partial(_sb_prompt_kernel, t=t),
        grid=(b, N_PAIRS // SB_CHAINS),
        in_specs=[blk, blk, blk, blk],
        out_specs=blk,
        out_shape=jax.ShapeDtypeStruct(q.shape, BF16),
        name="sb_prompt", compiler_params=_params(("arbitrary", "arbitrary")))(q, k, v, g)


def _sb_sample_kernel(q_ref, k_ref, v_ref, g_ref, *refs, ts, n_steps):
    ckt_refs, cvt_refs = refs[:N_PAIRS], refs[N_PAIRS:2 * N_PAIRS]
    o_ref, carry_ref, acc_ref = refs[2 * N_PAIRS:]
    step = pl.program_id(1)
    first = _first_head_lanes(ts)
    pairs = range(N_PAIRS)
    q2 = [_stack_heads(q_ref[0, p], first) for p in pairs]
    suffix = _suffix_matrix(SB_TK)

    def cached_tiles(chains, state=None):
        for i in reversed(range(SB_SAMPLE_KEYS // SB_TK)):
            keys = slice(i * SB_TK, (i + 1) * SB_TK)
            base = len(chains) - N_PAIRS
            chains = chains + [
                SbChain(lambda p=p, keys=keys: _nn_dot(q2[p], ckt_refs[p][0, :, keys].astype(BF16)),
                        lambda w, p=p, keys=keys: _nt_dot(w, cvt_refs[p][0, :, keys].astype(BF16)),
                        suffix, None, base + p if base >= 0 else None,
                        None if base >= 0 else state[p][0], None if base >= 0 else state[p][1])
                for p in pairs]
        return chains

    def run(chains):
        out = _sb_tiles(chains)[-N_PAIRS:]
        for p in pairs:
            carry_ref[p], acc_ref[p] = out[p]

    @pl.when(step == 0)
    def _():
        row = lax.broadcasted_iota(jnp.int32, (2 * ts, ts), 0)
        col = lax.broadcasted_iota(jnp.int32, (2 * ts, ts), 1)
        causal = col < jnp.bitwise_and(row, ts - 1)
        new_rows = [SbChain(lambda p=p: _nt_dot(q2[p], k_ref[0, p]), lambda w, p=p: _nn_dot(w, v_ref[0, p]),
                            _suffix_matrix(ts), causal) for p in pairs]
        run(cached_tiles(new_rows))

    @pl.when(step > 0)
    def _():
        run(cached_tiles([], [(carry_ref[p], acc_ref[p]) for p in pairs]))

    @pl.when(step == n_steps - 1)
    def _():
        for p in range(N_PAIRS):
            o_ref[0, p] = _gate(_unstack_heads(acc_ref[p], first), g_ref[0, p])


def _sb_sample_call(q, k, v, g, cache_kt, cache_vt, bs, ts):
    past = cache_kt.shape[2]
    assert past % SB_SAMPLE_KEYS == 0
    n_steps = past // SB_SAMPLE_KEYS
    blk = pl.BlockSpec((1, N_PAIRS, ts, PAIR_W), lambda b, s: (0, 0, b, 0))
    cblks = [pl.BlockSpec((1, PAIR_W, SB_SAMPLE_KEYS), lambda b, s, p=p: (b, p, n_steps - 1 - s))
             for p in range(N_PAIRS)]
    return pl.pallas_call(
        functools.partial(_sb_sample_kernel, ts=ts, n_steps=n_steps),
        grid=(bs, n_steps),
        in_specs=[blk, blk, blk, blk] + cblks + cblks,
        out_specs=blk,
        out_shape=jax.ShapeDtypeStruct(q.shape, BF16),
        scratch_shapes=[pltpu.VMEM((N_PAIRS, 2 * ts, 1), F32),
                        pltpu.VMEM((N_PAIRS, 2 * ts, PAIR_W), F32)],
        name="sb_sample", compiler_params=_params(("arbitrary", "arbitrary")))(
            q, k, v, g, *([cache_kt] * N_PAIRS), *([cache_vt] * N_PAIRS))


def kernel(x_prompt, x_sample, cache_a_k, cache_a_v, cache_b_k, cache_b_v, norm_a, w_in_a, rel_bias_a,
           w_out_a, norm_kv, w_kv, norm_b, w_in_b, w_out_b, norm_f):
    b, t, d = x_prompt.shape
    bs, ts, _ = x_sample.shape
    past = cache_b_k.shape[1]
    assert d == D_MODEL and norm_a.shape[0] == 1 and norm_b.shape[0] == 1
    assert cache_a_k.shape[2] == A_CACHE_ROWS and t % SB_TQ == 0 and t >= A_CACHE_ROWS
    assert past % CHUNK == 0 and ts == CHUNK

    w_a = w_in_a[0].astype(BF16)
    wo_a = w_out_a[0].astype(BF16)
    w_kvb = w_kv.astype(BF16)
    w_b = w_in_b[0].astype(BF16)
    wo_b = w_out_b[0].astype(BF16)
    g_a = norm_a
    g_b = jnp.stack([norm_kv, norm_b[0]])
    g_f = norm_f[None]
    xs = x_sample.reshape(1, bs * ts, d)

    time_minor = lambda c: jnp.transpose(c, (0, 2, 3, 1)).reshape(c.shape[0], d, c.shape[1])

    bias_p, bias_s = _bias_call(rel_bias_a[0], past, ts)

    plan_a = (Seg(0, 0, 0, SCALE * LOG2E, 0, None), Seg(0, 0, 1, 1.0, 1, 0),
              Seg(0, 0, 2, 1.0, 2, 1), Seg(0, 0, 3, 1.0, 3, None))
    plan_b = (Seg(0, 0, 0, 1.0, 0, 0), Seg(0, 0, 1, 1.0, 1, 1),
              Seg(1, 1, 0, SCALE * LOG2E, 2, None), Seg(1, 1, 1, 1.0, 3, None))

    qp, kp, vp, gp, akp, avp = _dense_call(x_prompt, gains=g_a, weights=(w_a,), plan=plan_a,
                                           f32_last_rows=A_CACHE_ROWS, name="proj_a_prompt")
    qs, ks, vs, gs, aks, avs = _dense_call(xs, gains=g_a, weights=(w_a,), plan=plan_a, f32_heads=True,
                                           name="proj_a_sample")

    ogp = _band_prompt_call(qp, kp, vp, gp, bias_p)
    ogs = _band_sample_call(qs, ks, vs, gs, time_minor(cache_a_k[0]), time_minor(cache_a_v[0]),
                            bias_s, bs, ts)

    xp1, kbp, vbp, qbp, gbp, kbp32, vbp32 = _dense_call(
        x_prompt, og=ogp, wo=wo_a, gains=g_b, weights=(w_kvb, w_b), plan=plan_b, emit_x=True,
        name="out_a_proj_b_prompt")
    xs1, kbs, vbs, qbs, gbs, kbs32, vbs32 = _dense_call(
        xs, og=ogs, wo=wo_a, gains=g_b, weights=(w_kvb, w_b), plan=plan_b, emit_x=True, f32_heads=True,
        name="out_a_proj_b_sample")

    obp = _sb_prompt_call(qbp, kbp, vbp, gbp)
    obs = _sb_sample_call(qbs, kbs, vbs, gbs, time_minor(cache_b_k), time_minor(cache_b_v), bs, ts)

    (y_prompt,) = _dense_call(xp1, og=obp, wo=wo_b, gf=g_f, name="out_b_prompt")
    (y_sample,) = _dense_call(xs1, og=obs, wo=wo_b, gf=g_f, name="out_b_sample")

    heads = lambda a, n, rows: a.reshape(n, rows, N_HEADS, HEAD_DIM)
    return (y_prompt, y_sample.reshape(bs, ts, d),
            heads(akp, b, A_CACHE_ROWS)[None], heads(avp, b, A_CACHE_ROWS)[None],
            heads(kbp32, b, t), heads(vbp32, b, t),
            heads(aks, bs, ts)[None], heads(avs, bs, ts)[None],
            heads(kbs32, bs, ts), heads(vbs32, bs, ts))
```

```python
import functools
from typing import NamedTuple, Optional

import jax
import jax.numpy as jnp
from jax import lax
from jax.experimental import pallas as pl
from jax.experimental.pallas import tpu as pltpu

D_MODEL = 1024
N_HEADS = 16
HEAD_DIM = 64
PAIR_W = 2 * HEAD_DIM
N_PAIRS = N_HEADS // 2
CHUNK = 64
LEFT_CHUNKS = 8
A_CACHE_ROWS = LEFT_CHUNKS * CHUNK
REL_CLIP = 128
N_REL = 2 * REL_CLIP + 1
N_REL_PAD = 384
RMS_EPS = 1e-6
NEG_INF = -1e30
SCALE = HEAD_DIM ** -0.5
LOG2E = 1.4426950408889634

BAND_TQ = 128
BAND_TK = A_CACHE_ROWS + BAND_TQ
BAND_VARIANTS = A_CACHE_ROWS // BAND_TQ + 1
BAND_SAMPLE_BATCH = 2
BAND_CHAINS = 16
SB_TQ = 256
SB_TK = 256
SB_CHAINS = 4
SB_SAMPLE_KEYS = 2048
SB_Z_CLAMP = 126.0

VMEM_LIMIT = 56 * 1024 * 1024

F32 = jnp.float32
BF16 = jnp.bfloat16


def _params(sem):
    return pltpu.CompilerParams(dimension_semantics=sem, vmem_limit_bytes=VMEM_LIMIT)


class Seg(NamedTuple):
    norm: int
    w: int
    col: int
    scale: float
    pm: Optional[int]
    f32: Optional[int]


def _dense_kernel(*refs, has_resid, emit_x, n_norm, n_w, plan, final_norm, n_pm, n_f32, f32_layout):
    it = iter(refs)
    x_ref = next(it)
    og_ref = next(it) if has_resid else None
    wo_ref = next(it) if has_resid else None
    g_ref = next(it) if n_norm else None
    w_refs = [next(it) for _ in range(n_w)]
    gf_ref = next(it) if final_norm else None
    xo_ref = next(it) if emit_x else None
    pm_refs = [next(it) for _ in range(n_pm)]
    f32_refs = [next(it) for _ in range(n_f32)]
    y_ref = next(it) if final_norm else None

    x = x_ref[0]
    if has_resid:
        og = jnp.concatenate([og_ref[0, p] for p in range(N_PAIRS)], axis=1)
        x = x + jnp.dot(og, wo_ref[...], preferred_element_type=F32)
        if emit_x:
            xo_ref[0] = x
    if n_norm or final_norm:
        xn = x * lax.rsqrt(jnp.mean(x * x, axis=-1, keepdims=True) + RMS_EPS)
    if final_norm:
        y_ref[0] = xn * gf_ref[...]
    hs = [(xn * g_ref[i:i + 1, :]).astype(BF16) for i in range(n_norm)]
    for seg in plan:
        w = w_refs[seg.w][:, seg.col * D_MODEL:(seg.col + 1) * D_MODEL]
        acc = jnp.dot(hs[seg.norm], w, preferred_element_type=F32)
        if seg.f32 is not None:
            if f32_layout == "heads":
                f32_refs[seg.f32][0] = pltpu.einshape("r(hd)->rhd", acc, h=N_HEADS)
            elif f32_layout == "time_minor":
                f32_refs[seg.f32][0] = acc.T
            else:
                f32_refs[seg.f32][0] = acc
        if seg.pm is not None:
            ab = (acc * seg.scale).astype(BF16) if seg.scale != 1.0 else acc.astype(BF16)
            for p in range(N_PAIRS):
                pm_refs[seg.pm][0, p] = ab[:, p * PAIR_W:(p + 1) * PAIR_W]


def _dense_call(x, *, og=None, wo=None, gains=None, weights=(), plan=(), gf=None,
                emit_x=False, f32_last_rows=None, f32_layout="rows", tm=512, name="dense"):
    bx, tx, d = x.shape
    assert d == D_MODEL and tx % tm == 0
    has_resid = og is not None
    n_norm = 0 if gains is None else gains.shape[0]
    n_pm = sum(s.pm is not None for s in plan)
    n_f32 = sum(s.f32 is not None for s in plan)
    final_norm = gf is not None

    row_spec = pl.BlockSpec((1, tm, d), lambda b, t: (b, t, 0))
    pm_spec = pl.BlockSpec((1, N_PAIRS, tm, PAIR_W), lambda b, t: (b, 0, t, 0))
    whole = lambda a: pl.BlockSpec(a.shape, lambda b, t: (0,) * a.ndim, pipeline_mode=pl.Buffered(1))

    in_arrays, in_specs = [x], [row_spec]
    if has_resid:
        in_arrays += [og, wo]
        in_specs += [pm_spec, whole(wo)]
    if n_norm:
        in_arrays.append(gains)
        in_specs.append(whole(gains))
    for w in weights:
        in_arrays.append(w)
        in_specs.append(whole(w))
    if final_norm:
        in_arrays.append(gf)
        in_specs.append(whole(gf))

    out_shapes, out_specs = [], []
    if emit_x:
        out_shapes.append(jax.ShapeDtypeStruct((bx, tx, d), F32))
        out_specs.append(row_spec)
    for _ in range(n_pm):
        out_shapes.append(jax.ShapeDtypeStruct((bx, N_PAIRS, tx, PAIR_W), BF16))
        out_specs.append(pm_spec)
    assert f32_last_rows in (None, tm)
    rows_kept = tx if f32_last_rows is None else tm
    keep = (lambda t: t) if f32_last_rows is None else (lambda t: 0)
    for _ in range(n_f32):
        if f32_layout == "heads":
            out_shapes.append(jax.ShapeDtypeStruct((bx, rows_kept, N_HEADS, HEAD_DIM), F32))
            out_specs.append(pl.BlockSpec((1, tm, N_HEADS, HEAD_DIM), lambda b, t: (b, keep(t), 0, 0)))
        elif f32_layout == "time_minor":
            out_shapes.append(jax.ShapeDtypeStruct((bx, d, rows_kept), F32))
            out_specs.append(pl.BlockSpec((1, d, tm), lambda b, t: (b, 0, keep(t))))
        else:
            out_shapes.append(jax.ShapeDtypeStruct((bx, rows_kept, d), F32))
            out_specs.append(pl.BlockSpec((1, tm, d), lambda b, t: (b, keep(t), 0)))
    if final_norm:
        out_shapes.append(jax.ShapeDtypeStruct((bx, tx, d), F32))
        out_specs.append(row_spec)

    body = functools.partial(
        _dense_kernel, has_resid=has_resid, emit_x=emit_x, n_norm=n_norm, n_w=len(weights),
        plan=tuple(plan), final_norm=final_norm, n_pm=n_pm, n_f32=n_f32, f32_layout=f32_layout)
    return pl.pallas_call(
        body, grid=(bx, tx // tm), in_specs=in_specs, out_specs=out_specs, out_shape=out_shapes,
        name=name, compiler_params=_params(("arbitrary", "arbitrary")))(*in_arrays)


def _bias_kernel(rb_ref, bp_ref, bs_ref, *, past_len, ts):
    width = 768
    d0 = A_CACHE_ROWS
    rb = rb_ref[0]
    hi = rb.astype(BF16)
    r1 = rb - hi.astype(F32)
    mid = r1.astype(BF16)
    lo = (r1 - mid.astype(F32)).astype(BF16)
    c = lax.broadcasted_iota(jnp.int32, (N_REL_PAD, width), 0)
    n = lax.broadcasted_iota(jnp.int32, (N_REL_PAD, width), 1)
    m = jnp.where(n < BAND_TK, n, n - width)
    tgt = jnp.clip(d0 - m, -REL_CLIP, REL_CLIP) + REL_CLIP
    onehot = jnp.where(c == tgt, 1.0, 0.0).astype(BF16)
    r_ext = (jnp.dot(hi, onehot, preferred_element_type=F32)
             + jnp.dot(mid, onehot, preferred_element_type=F32)
             + jnp.dot(lo, onehot, preferred_element_type=F32))

    sub = lax.broadcasted_iota(jnp.int32, (8, width), 0)
    nks = A_CACHE_ROWS + ts
    for h2 in range(2):
        base = jnp.broadcast_to(r_ext[h2:h2 + 1, :], (8, width))
        b8 = base
        for r in range(1, 8):
            b8 = jnp.where(sub == r, pltpu.roll(base, r, 1), b8)
        blocks = [b8] + [pltpu.roll(b8, 8 * gi, 1) for gi in range(1, BAND_TQ // 8)]
        toep = jnp.concatenate(blocks, axis=0)[:, :BAND_TK]

        def masked(q0, k0, rows, cols):
            qpos = q0 + lax.broadcasted_iota(jnp.int32, (rows, cols), 0)
            kpos = k0 + lax.broadcasted_iota(jnp.int32, (rows, cols), 1)
            qc0 = qpos - jnp.bitwise_and(qpos, CHUNK - 1)
            lo_k = jnp.maximum(qc0 - A_CACHE_ROWS, 0)
            ok = jnp.logical_and(kpos >= lo_k, kpos < qc0 + CHUNK)
            return jnp.where(ok, toep[:rows, :cols] * LOG2E, NEG_INF)

        for v in range(BAND_VARIANTS):
            q0 = BAND_TQ * v
            bp_ref[0, v, h2 * BAND_TQ:(h2 + 1) * BAND_TQ, :] = masked(q0, q0 - A_CACHE_ROWS, BAND_TQ, BAND_TK)
        bs_ref[0, h2 * ts:(h2 + 1) * ts, :] = masked(past_len, past_len - A_CACHE_ROWS, ts, nks)


def _bias_call(rel_bias, past_len, ts):
    rb = jnp.pad(rel_bias.reshape(N_PAIRS, 2, N_REL), ((0, 0), (0, 6), (0, N_REL_PAD - N_REL)))
    nks = A_CACHE_ROWS + ts
    return pl.pallas_call(
        functools.partial(_bias_kernel, past_len=past_len, ts=ts),
        grid=(N_PAIRS,),
        in_specs=[pl.BlockSpec((1, 8, N_REL_PAD), lambda p: (p, 0, 0))],
        out_specs=[pl.BlockSpec((1, BAND_VARIANTS, 2 * BAND_TQ, BAND_TK), lambda p: (p, 0, 0, 0)),
                   pl.BlockSpec((1, 2 * ts, nks), lambda p: (p, 0, 0))],
        out_shape=[jax.ShapeDtypeStruct((N_PAIRS, BAND_VARIANTS, 2 * BAND_TQ, BAND_TK), F32),
                   jax.ShapeDtypeStruct((N_PAIRS, 2 * ts, nks), F32)],
        name="band_bias", compiler_params=_params(("arbitrary",)))(rb)


def _first_head_lanes(rows):
    return lax.broadcasted_iota(jnp.int32, (rows, PAIR_W), 1) < HEAD_DIM


def _stack_heads(q, first):
    zero = jnp.zeros_like(q)
    return jnp.concatenate([jnp.where(first, q, zero), jnp.where(first, zero, q)], axis=0)


def _unstack_heads(o2, first):
    rows = o2.shape[0] // 2
    return jnp.where(first, o2[:rows], o2[rows:])


def _nt_dot(a, b):
    return lax.dot_general(a, b, (((1,), (1,)), ((), ())), preferred_element_type=F32)


def _gate(o, g):
    g = g.astype(F32)
    return (o * (g * jax.nn.sigmoid(g))).astype(BF16)


def _nn_dot(a, b):
    return jnp.dot(a, b, preferred_element_type=F32)


def _skewed(n_chains, n_stages, stage):
    for slot in range(n_stages + n_chains - 1):
        for c in range(n_chains):
            if 0 <= slot - c < n_stages:
                stage(c, slot - c)


def _softmax_chains(score_fns, pv_fns):
    n = len(score_fns)
    st = [dict() for _ in range(n)]

    def stage(c, s):
        d = st[c]
        if s == 0:
            d["s"] = score_fns[c]()
        elif s == 1:
            d["m"] = jnp.max(d["s"], axis=-1, keepdims=True)
        elif s == 2:
            p = jnp.exp2(d.pop("s") - d.pop("m"))
            d["l"] = jnp.sum(p, axis=-1, keepdims=True)
            d["p"] = p.astype(BF16)
        elif s == 3:
            d["o"] = pv_fns[c](d.pop("p"))
        else:
            d["o"] = d["o"] * (1.0 / d.pop("l"))

    _skewed(n, 5, stage)
    return [d["o"] for d in st]


def _band_prompt_kernel(q_ref, k_ref, v_ref, g_ref, bias_ref, o_ref, kpad, vpad, *, t):
    zeros = jnp.zeros((A_CACHE_ROWS, PAIR_W), BF16)
    kpad[0:A_CACHE_ROWS, :] = zeros
    vpad[0:A_CACHE_ROWS, :] = zeros
    kpad[A_CACHE_ROWS:, :] = k_ref[0, 0]
    vpad[A_CACHE_ROWS:, :] = v_ref[0, 0]
    first = _first_head_lanes(BAND_TQ)

    def body(i, carry):
        qts = [i * BAND_CHAINS + c for c in range(BAND_CHAINS)]
        r0s = [pl.multiple_of(qt * BAND_TQ, BAND_TQ) for qt in qts]
        def score_fn(qt, r0):
            def fn():
                q2 = _stack_heads(q_ref[0, 0, pl.ds(r0, BAND_TQ), :], first)
                kw = kpad[pl.ds(r0, BAND_TK), :]
                return _nt_dot(q2, kw) + bias_ref[0, jnp.minimum(qt, BAND_VARIANTS - 1)]
            return fn

        outs = _softmax_chains(
            [score_fn(qt, r0) for qt, r0 in zip(qts, r0s)],
            [lambda p, r0=r0: _nn_dot(p, vpad[pl.ds(r0, BAND_TK), :]) for r0 in r0s])
        for r0, o2 in zip(r0s, outs):
            o = _unstack_heads(o2, first)
            o_ref[0, 0, pl.ds(r0, BAND_TQ), :] = _gate(o, g_ref[0, 0, pl.ds(r0, BAND_TQ), :])
        return carry

    assert (t // BAND_TQ) % BAND_CHAINS == 0
    lax.fori_loop(0, t // BAND_TQ // BAND_CHAINS, body, 0)


def _band_prompt_call(q, k, v, g, bias_p):
    b, _, t, _ = q.shape
    blk = pl.BlockSpec((1, 1, t, PAIR_W), lambda p, bb: (bb, p, 0, 0))
    return pl.pallas_call(
        functools.partial(_band_prompt_kernel, t=t),
        grid=(N_PAIRS, b),
        in_specs=[blk, blk, blk, blk,
                  pl.BlockSpec((1, BAND_VARIANTS, 2 * BAND_TQ, BAND_TK), lambda p, bb: (p, 0, 0, 0))],
        out_specs=blk,
        out_shape=jax.ShapeDtypeStruct(q.shape, BF16),
        scratch_shapes=[pltpu.VMEM((t + A_CACHE_ROWS, PAIR_W), BF16),
                        pltpu.VMEM((t + A_CACHE_ROWS, PAIR_W), BF16)],
        name="band_prompt", compiler_params=_params(("arbitrary", "arbitrary")))(q, k, v, g, bias_p)


def _band_sample_kernel(q_ref, k_ref, v_ref, g_ref, *refs, ts):
    ckt_refs, cvt_refs = refs[:N_PAIRS], refs[N_PAIRS:2 * N_PAIRS]
    bias_ref, o_ref = refs[2 * N_PAIRS:]
    first = _first_head_lanes(ts)
    units = [(i, p) for i in range(BAND_SAMPLE_BATCH) for p in range(N_PAIRS)]
    new = lambda i: slice(i * ts, (i + 1) * ts)

    def score_fn(i, p):
        def fn():
            q2 = _stack_heads(q_ref[0, p, new(i)], first)
            s_old = _nn_dot(q2, ckt_refs[p][i].astype(BF16))
            s_new = _nt_dot(q2, k_ref[0, p, new(i)])
            return jnp.concatenate([s_old, s_new], axis=1) + bias_ref[p]
        return fn

    def pv_fn(i, p):
        def fn(prob):
            return (_nt_dot(prob[:, :A_CACHE_ROWS], cvt_refs[p][i].astype(BF16))
                    + _nn_dot(prob[:, A_CACHE_ROWS:], v_ref[0, p, new(i)]))
        return fn

    outs = _softmax_chains([score_fn(i, p) for i, p in units], [pv_fn(i, p) for i, p in units])
    for (i, p), o2 in zip(units, outs):
        o_ref[0, p, new(i)] = _gate(_unstack_heads(o2, first), g_ref[0, p, new(i)])


def _band_sample_call(q, k, v, g, cache_kt, cache_vt, bias_s, bs, ts):
    nb = BAND_SAMPLE_BATCH
    assert bs % nb == 0
    blk = pl.BlockSpec((1, N_PAIRS, nb * ts, PAIR_W), lambda b: (0, 0, b, 0))
    cblks = [pl.BlockSpec((nb, PAIR_W, A_CACHE_ROWS), lambda b, p=p: (b, p, 0)) for p in range(N_PAIRS)]
    return pl.pallas_call(
        functools.partial(_band_sample_kernel, ts=ts),
        grid=(bs // nb,),
        in_specs=[blk, blk, blk, blk] + cblks + cblks + [pl.BlockSpec(bias_s.shape, lambda b: (0, 0, 0))],
        out_specs=blk,
        out_shape=jax.ShapeDtypeStruct(q.shape, BF16),
        name="band_sample", compiler_params=_params(("arbitrary",)))(
            q, k, v, g, *([cache_kt] * N_PAIRS), *([cache_vt] * N_PAIRS), bias_s)


def _suffix_matrix(n):
    r = lax.broadcasted_iota(jnp.int32, (n, n), 0)
    c = lax.broadcasted_iota(jnp.int32, (n, n), 1)
    return jnp.where(r > c, 1.0, 0.0).astype(BF16)


class SbChain(NamedTuple):
    score_fn: object
    pv_fn: object
    suffix: jax.Array
    causal: Optional[jax.Array] = None
    prev: Optional[int] = None
    carry: Optional[jax.Array] = None
    acc: Optional[jax.Array] = None


def _sb_tiles(chains):
    st = [dict() for _ in chains]

    def stage(c, s):
        ch, d = chains[c], st[c]
        if s == 0:
            d["z"] = ch.score_fn()
        elif s == 1:
            z = d.pop("z")
            sp = jnp.maximum(jnp.log2(1.0 + jnp.exp2(jnp.minimum(z, SB_Z_CLAMP))), z)
            if ch.causal is not None:
                sp = jnp.where(ch.causal, sp, 0.0)
            total = jnp.sum(sp, axis=-1, keepdims=True)
            carry = ch.carry if ch.prev is None else st[ch.prev]["carry"]
            d["sp"] = sp.astype(BF16)
            d["zs"] = z - sp if carry is None else (z - sp) - carry
            d["carry"] = total if carry is None else carry + total
        elif s == 2:
            d["later"] = _nn_dot(d.pop("sp"), ch.suffix)
        elif s == 3:
            w = jnp.exp2(d.pop("zs") - d.pop("later"))
            if ch.causal is not None:
                w = jnp.where(ch.causal, w, 0.0)
            d["w"] = w.astype(BF16)
        else:
            pv = ch.pv_fn(d.pop("w"))
            acc = ch.acc if ch.prev is None else st[ch.prev]["acc"]
            d["acc"] = pv if acc is None else acc + pv

    _skewed(len(chains), 5, stage)
    return [(d["carry"], d["acc"]) for d in st]


def _sb_prompt_kernel(q_ref, k_ref, v_ref, g_ref, o_ref, *, t):
    first = _first_head_lanes(SB_TQ)
    suffix = _suffix_matrix(SB_TK)
    row = lax.broadcasted_iota(jnp.int32, (SB_TQ, SB_TK), 0)
    col = lax.broadcasted_iota(jnp.int32, (SB_TQ, SB_TK), 1)
    causal = col < row

    pairs = range(SB_CHAINS)
    chain_pair = [p for p in pairs for _ in range(2)]

    n = len(chain_pair)

    def q_body(qt, c0):
        r0 = pl.multiple_of(qt * SB_TQ, SB_TQ)
        qh = []
        for p in pairs:
            q = q_ref[0, p, pl.ds(r0, SB_TQ), :]
            zero = jnp.zeros_like(q)
            qh += [jnp.where(first, q, zero), jnp.where(first, zero, q)]

        def tile(k0, mask=None, state=None, prev0=None):
            return [SbChain(lambda c=c, p=p: _nt_dot(qh[c], k_ref[0, p, pl.ds(k0, SB_TK), :]),
                            lambda w, p=p: _nn_dot(w, v_ref[0, p, pl.ds(k0, SB_TK), :]),
                            suffix, mask, None if prev0 is None else prev0 + c,
                            None if state is None else state[2 * c],
                            None if state is None else state[2 * c + 1])
                    for c, p in enumerate(chain_pair)]

        def flat(results):
            return tuple(x for r in results for x in r)

        def own_tile_alone():
            return flat(_sb_tiles(tile(r0, causal)))

        def own_tile_and_next():
            k1 = pl.multiple_of(r0 - SB_TK, SB_TK)
            return flat(_sb_tiles(tile(r0, causal) + tile(k1, prev0=0))[n:])

        odd = jnp.bitwise_and(qt, 1)
        state = lax.cond(odd == 1, own_tile_and_next, own_tile_alone)

        def k_body(i, st):
            ka = pl.multiple_of((qt - 1 - odd - 2 * i) * SB_TK, SB_TK)
            kb = pl.multiple_of(ka - SB_TK, SB_TK)
            return flat(_sb_tiles(tile(ka, state=st) + tile(kb, prev0=0))[n:])

        state = lax.fori_loop(0, lax.shift_right_logical(qt, 1), k_body, state)
        for p in pairs:
            o = jnp.where(first, state[4 * p + 1], state[4 * p + 3])
            o_ref[0, p, pl.ds(r0, SB_TQ), :] = _gate(o, g_ref[0, p, pl.ds(r0, SB_TQ), :])
        return c0

    lax.fori_loop(0, t // SB_TQ, q_body, 0)


def _sb_prompt_call(q, k, v, g):
    b, _, t, _ = q.shape
    blk = pl.BlockSpec((1, SB_CHAINS, t, PAIR_W), lambda bb, p: (bb, p, 0, 0))
    return pl.pallas_call(
        functools.partial(_sb_prompt_kernel, t=t),
        grid=(b, N_PAIRS // SB_CHAINS),
        in_specs=[blk, blk, blk, blk],
        out_specs=blk,
        out_shape=jax.ShapeDtypeStruct(q.shape, BF16),
        name="sb_prompt", compiler_params=_params(("arbitrary", "arbitrary")))(q, k, v, g)


def _sb_sample_kernel(q_ref, k_ref, v_ref, g_ref, *refs, ts, n_steps):
    ckt_refs, cvt_refs = refs[:N_PAIRS], refs[N_PAIRS:2 * N_PAIRS]
    o_ref, carry_ref, acc_ref = refs[2 * N_PAIRS:]
    step = pl.program_id(1)
    first = _first_head_lanes(ts)
    pairs = range(N_PAIRS)
    q2 = [_stack_heads(q_ref[0, p], first) for p in pairs]
    suffix = _suffix_matrix(SB_TK)

    def cached_tiles(chains, state=None):
        for i in reversed(range(SB_SAMPLE_KEYS // SB_TK)):
            keys = slice(i * SB_TK, (i + 1) * SB_TK)
            base = len(chains) - N_PAIRS
            chains = chains + [
                SbChain(lambda p=p, keys=keys: _nn_dot(q2[p], ckt_refs[p][0, :, keys].astype(BF16)),
                        lambda w, p=p, keys=keys: _nt_dot(w, cvt_refs[p][0, :, keys].astype(BF16)),
                        suffix, None, base + p if base >= 0 else None,
                        None if base >= 0 else state[p][0], None if base >= 0 else state[p][1])
                for p in pairs]
        return chains

    def run(chains):
        out = _sb_tiles(chains)[-N_PAIRS:]
        for p in pairs:
            carry_ref[p], acc_ref[p] = out[p]

    @pl.when(step == 0)
    def _():
        row = lax.broadcasted_iota(jnp.int32, (2 * ts, ts), 0)
        col = lax.broadcasted_iota(jnp.int32, (2 * ts, ts), 1)
        causal = col < jnp.bitwise_and(row, ts - 1)
        new_rows = [SbChain(lambda p=p: _nt_dot(q2[p], k_ref[0, p]), lambda w, p=p: _nn_dot(w, v_ref[0, p]),
                            _suffix_matrix(ts), causal) for p in pairs]
        run(cached_tiles(new_rows))

    @pl.when(step > 0)
    def _():
        run(cached_tiles([], [(carry_ref[p], acc_ref[p]) for p in pairs]))

    @pl.when(step == n_steps - 1)
    def _():
        for p in range(N_PAIRS):
            o_ref[0, p] = _gate(_unstack_heads(acc_ref[p], first), g_ref[0, p])


def _sb_sample_call(q, k, v, g, cache_kt, cache_vt, bs, ts):
    past = cache_kt.shape[2]
    assert past % SB_SAMPLE_KEYS == 0
    n_steps = past // SB_SAMPLE_KEYS
    blk = pl.BlockSpec((1, N_PAIRS, ts, PAIR_W), lambda b, s: (0, 0, b, 0))
    cblks = [pl.BlockSpec((1, PAIR_W, SB_SAMPLE_KEYS), lambda b, s, p=p: (b, p, n_steps - 1 - s))
             for p in range(N_PAIRS)]
    return pl.pallas_call(
        functools.partial(_sb_sample_kernel, ts=ts, n_steps=n_steps),
        grid=(bs, n_steps),
        in_specs=[blk, blk, blk, blk] + cblks + cblks,
        out_specs=blk,
        out_shape=jax.ShapeDtypeStruct(q.shape, BF16),
        scratch_shapes=[pltpu.VMEM((N_PAIRS, 2 * ts, 1), F32),
                        pltpu.VMEM((N_PAIRS, 2 * ts, PAIR_W), F32)],
        name="sb_sample", compiler_params=_params(("arbitrary", "arbitrary")))(
            q, k, v, g, *([cache_kt] * N_PAIRS), *([cache_vt] * N_PAIRS))


def kernel(x_prompt, x_sample, cache_a_k, cache_a_v, cache_b_k, cache_b_v, norm_a, w_in_a, rel_bias_a,
           w_out_a, norm_kv, w_kv, norm_b, w_in_b, w_out_b, norm_f):
    b, t, d = x_prompt.shape
    bs, ts, _ = x_sample.shape
    past = cache_b_k.shape[1]
    assert d == D_MODEL and norm_a.shape[0] == 1 and norm_b.shape[0] == 1
    assert cache_a_k.shape[2] == A_CACHE_ROWS and t % SB_TQ == 0 and t >= A_CACHE_ROWS
    assert past % CHUNK == 0 and ts == CHUNK

    w_a = w_in_a[0].astype(BF16)
    wo_a = w_out_a[0].astype(BF16)
    w_kvb = w_kv.astype(BF16)
    w_b = w_in_b[0].astype(BF16)
    wo_b = w_out_b[0].astype(BF16)
    g_a = norm_a
    g_b = jnp.stack([norm_kv, norm_b[0]])
    g_f = norm_f[None]
    xs = x_sample.reshape(1, bs * ts, d)

    time_minor = lambda c: jnp.transpose(c, (0, 2, 3, 1)).reshape(c.shape[0], d, c.shape[1])

    bias_p, bias_s = _bias_call(rel_bias_a[0], past, ts)

    plan_a = (Seg(0, 0, 0, SCALE * LOG2E, 0, None), Seg(0, 0, 1, 1.0, 1, 0),
              Seg(0, 0, 2, 1.0, 2, 1), Seg(0, 0, 3, 1.0, 3, None))
    plan_b = (Seg(0, 0, 0, 1.0, 0, 0), Seg(0, 0, 1, 1.0, 1, 1),
              Seg(1, 1, 0, SCALE * LOG2E, 2, None), Seg(1, 1, 1, 1.0, 3, None))

    qp, kp, vp, gp, akp, avp = _dense_call(x_prompt, gains=g_a, weights=(w_a,), plan=plan_a,
                                           f32_last_rows=A_CACHE_ROWS, f32_layout="time_minor",
                                           name="proj_a_prompt")
    qs, ks, vs, gs, aks, avs = _dense_call(xs, gains=g_a, weights=(w_a,), plan=plan_a, f32_layout="heads",
                                           name="proj_a_sample")

    ogp = _band_prompt_call(qp, kp, vp, gp, bias_p)
    ogs = _band_sample_call(qs, ks, vs, gs, time_minor(cache_a_k[0]), time_minor(cache_a_v[0]),
                            bias_s, bs, ts)

    xp1, kbp, vbp, qbp, gbp, kbp32, vbp32 = _dense_call(
        x_prompt, og=ogp, wo=wo_a, gains=g_b, weights=(w_kvb, w_b), plan=plan_b, emit_x=True,
        f32_layout="time_minor", name="out_a_proj_b_prompt")
    xs1, kbs, vbs, qbs, gbs, kbs32, vbs32 = _dense_call(
        xs, og=ogs, wo=wo_a, gains=g_b, weights=(w_kvb, w_b), plan=plan_b, emit_x=True, f32_layout="heads",
        name="out_a_proj_b_sample")

    obp = _sb_prompt_call(qbp, kbp, vbp, gbp)
    obs = _sb_sample_call(qbs, kbs, vbs, gbs, time_minor(cache_b_k), time_minor(cache_b_v), bs, ts)

    (y_prompt,) = _dense_call(xp1, og=obp, wo=wo_b, gf=g_f, name="out_b_prompt")
    (y_sample,) = _dense_call(xs1, og=obs, wo=wo_b, gf=g_f, name="out_b_sample")

    heads = lambda a, n, rows: a.reshape(n, rows, N_HEADS, HEAD_DIM)
    from_time_minor = lambda a: jnp.transpose(a.reshape(a.shape[0], N_HEADS, HEAD_DIM, a.shape[2]), (0, 3, 1, 2))
    return (y_prompt, y_sample.reshape(bs, ts, d),
            from_time_minor(akp)[None], from_time_minor(avp)[None],
            from_time_minor(kbp32), from_time_minor(vbp32),
            heads(aks, bs, ts)[None], heads(avs, bs, ts)[None],
            heads(kbs32, bs, ts), heads(vbs32, bs, ts))
```

```python
import functools
from typing import NamedTuple, Optional

import jax
import jax.numpy as jnp
from jax import lax
from jax.experimental import pallas as pl
from jax.experimental.pallas import tpu as pltpu

D_MODEL = 1024
N_HEADS = 16
HEAD_DIM = 64
PAIR_W = 2 * HEAD_DIM
N_PAIRS = N_HEADS // 2
CHUNK = 64
LEFT_CHUNKS = 8
A_CACHE_ROWS = LEFT_CHUNKS * CHUNK
REL_CLIP = 128
N_REL = 2 * REL_CLIP + 1
N_REL_PAD = 384
RMS_EPS = 1e-6
NEG_INF = -1e30
SCALE = HEAD_DIM ** -0.5
LOG2E = 1.4426950408889634

BAND_TQ = 128
BAND_TK = A_CACHE_ROWS + BAND_TQ
BAND_VARIANTS = A_CACHE_ROWS // BAND_TQ + 1
BAND_SAMPLE_BATCH = 2
BAND_CHAINS = 16
SB_TQ = 256
SB_TK = 256
SB_CHAINS = 4
SB_SAMPLE_KEYS = 2048
SB_Z_CLAMP = 126.0

VMEM_LIMIT = 56 * 1024 * 1024

F32 = jnp.float32
BF16 = jnp.bfloat16


def _params(sem):
    return pltpu.CompilerParams(dimension_semantics=sem, vmem_limit_bytes=VMEM_LIMIT)


class Seg(NamedTuple):
    norm: int
    w: int
    col: int
    scale: float
    pm: Optional[int]
    f32: Optional[int]


def _dense_kernel(*refs, has_resid, emit_x, n_norm, n_w, plan, final_norm, n_pm, n_f32, f32_layout):
    it = iter(refs)
    x_ref = next(it)
    og_ref = next(it) if has_resid else None
    wo_ref = next(it) if has_resid else None
    g_ref = next(it) if n_norm else None
    w_refs = [next(it) for _ in range(n_w)]
    gf_ref = next(it) if final_norm else None
    xo_ref = next(it) if emit_x else None
    pm_refs = [next(it) for _ in range(n_pm)]
    f32_refs = [next(it) for _ in range(n_f32)]
    y_ref = next(it) if final_norm else None

    x = x_ref[0]
    if has_resid:
        og = jnp.concatenate([og_ref[0, p] for p in range(N_PAIRS)], axis=1)
        x = x + jnp.dot(og, wo_ref[...], preferred_element_type=F32)
        if emit_x:
            xo_ref[0] = x
    if n_norm or final_norm:
        xn = x * lax.rsqrt(jnp.mean(x * x, axis=-1, keepdims=True) + RMS_EPS)
    if final_norm:
        y_ref[0] = xn * gf_ref[...]
    hs = [(xn * g_ref[i:i + 1, :]).astype(BF16) for i in range(n_norm)]
    for seg in plan:
        w = w_refs[seg.w][:, seg.col * D_MODEL:(seg.col + 1) * D_MODEL]
        acc = jnp.dot(hs[seg.norm], w, preferred_element_type=F32)
        if seg.f32 is not None:
            if f32_layout == "heads":
                f32_refs[seg.f32][0] = pltpu.einshape("r(hd)->rhd", acc, h=N_HEADS)
            elif f32_layout == "time_minor":
                f32_refs[seg.f32][0] = acc.T
            else:
                f32_refs[seg.f32][0] = acc
        if seg.pm is not None:
            ab = (acc * seg.scale).astype(BF16) if seg.scale != 1.0 else acc.astype(BF16)
            for p in range(N_PAIRS):
                pm_refs[seg.pm][0, p] = ab[:, p * PAIR_W:(p + 1) * PAIR_W]


def _dense_call(x, *, og=None, wo=None, gains=None, weights=(), plan=(), gf=None,
                emit_x=False, f32_last_rows=None, f32_layout="rows", tm=512, name="dense"):
    bx, tx, d = x.shape
    assert d == D_MODEL and tx % tm == 0
    has_resid = og is not None
    n_norm = 0 if gains is None else gains.shape[0]
    n_pm = sum(s.pm is not None for s in plan)
    n_f32 = sum(s.f32 is not None for s in plan)
    final_norm = gf is not None

    row_spec = pl.BlockSpec((1, tm, d), lambda b, t: (b, t, 0))
    pm_spec = pl.BlockSpec((1, N_PAIRS, tm, PAIR_W), lambda b, t: (b, 0, t, 0))
    whole = lambda a: pl.BlockSpec(a.shape, lambda b, t: (0,) * a.ndim, pipeline_mode=pl.Buffered(1))

    in_arrays, in_specs = [x], [row_spec]
    if has_resid:
        in_arrays += [og, wo]
        in_specs += [pm_spec, whole(wo)]
    if n_norm:
        in_arrays.append(gains)
        in_specs.append(whole(gains))
    for w in weights:
        in_arrays.append(w)
        in_specs.append(whole(w))
    if final_norm:
        in_arrays.append(gf)
        in_specs.append(whole(gf))

    out_shapes, out_specs = [], []
    if emit_x:
        out_shapes.append(jax.ShapeDtypeStruct((bx, tx, d), F32))
        out_specs.append(row_spec)
    for _ in range(n_pm):
        out_shapes.append(jax.ShapeDtypeStruct((bx, N_PAIRS, tx, PAIR_W), BF16))
        out_specs.append(pm_spec)
    assert f32_last_rows in (None, tm)
    rows_kept = tx if f32_last_rows is None else tm
    keep = (lambda t: t) if f32_last_rows is None else (lambda t: 0)
    for _ in range(n_f32):
        if f32_layout == "heads":
            out_shapes.append(jax.ShapeDtypeStruct((bx, rows_kept, N_HEADS, HEAD_DIM), F32))
            out_specs.append(pl.BlockSpec((1, tm, N_HEADS, HEAD_DIM), lambda b, t: (b, keep(t), 0, 0)))
        elif f32_layout == "time_minor":
            out_shapes.append(jax.ShapeDtypeStruct((bx, d, rows_kept), F32))
            out_specs.append(pl.BlockSpec((1, d, tm), lambda b, t: (b, 0, keep(t))))
        else:
            out_shapes.append(jax.ShapeDtypeStruct((bx, rows_kept, d), F32))
            out_specs.append(pl.BlockSpec((1, tm, d), lambda b, t: (b, keep(t), 0)))
    if final_norm:
        out_shapes.append(jax.ShapeDtypeStruct((bx, tx, d), F32))
        out_specs.append(row_spec)

    body = functools.partial(
        _dense_kernel, has_resid=has_resid, emit_x=emit_x, n_norm=n_norm, n_w=len(weights),
        plan=tuple(plan), final_norm=final_norm, n_pm=n_pm, n_f32=n_f32, f32_layout=f32_layout)
    return pl.pallas_call(
        body, grid=(bx, tx // tm), in_specs=in_specs, out_specs=out_specs, out_shape=out_shapes,
        name=name, compiler_params=_params(("arbitrary", "arbitrary")))(*in_arrays)


def _bias_kernel(rb_ref, bp_ref, bs_ref, *, past_len, ts):
    width = 768
    d0 = A_CACHE_ROWS
    rb = rb_ref[0]
    hi = rb.astype(BF16)
    r1 = rb - hi.astype(F32)
    mid = r1.astype(BF16)
    lo = (r1 - mid.astype(F32)).astype(BF16)
    c = lax.broadcasted_iota(jnp.int32, (N_REL_PAD, width), 0)
    n = lax.broadcasted_iota(jnp.int32, (N_REL_PAD, width), 1)
    m = jnp.where(n < BAND_TK, n, n - width)
    tgt = jnp.clip(d0 - m, -REL_CLIP, REL_CLIP) + REL_CLIP
    onehot = jnp.where(c == tgt, 1.0, 0.0).astype(BF16)
    r_ext = (jnp.dot(hi, onehot, preferred_element_type=F32)
             + jnp.dot(mid, onehot, preferred_element_type=F32)
             + jnp.dot(lo, onehot, preferred_element_type=F32))

    sub = lax.broadcasted_iota(jnp.int32, (8, width), 0)
    nks = A_CACHE_ROWS + ts
    for h2 in range(2):
        base = jnp.broadcast_to(r_ext[h2:h2 + 1, :], (8, width))
        b8 = base
        for r in range(1, 8):
            b8 = jnp.where(sub == r, pltpu.roll(base, r, 1), b8)
        blocks = [b8] + [pltpu.roll(b8, 8 * gi, 1) for gi in range(1, BAND_TQ // 8)]
        toep = jnp.concatenate(blocks, axis=0)[:, :BAND_TK]

        def masked(q0, k0, rows, cols):
            qpos = q0 + lax.broadcasted_iota(jnp.int32, (rows, cols), 0)
            kpos = k0 + lax.broadcasted_iota(jnp.int32, (rows, cols), 1)
            qc0 = qpos - jnp.bitwise_and(qpos, CHUNK - 1)
            lo_k = jnp.maximum(qc0 - A_CACHE_ROWS, 0)
            ok = jnp.logical_and(kpos >= lo_k, kpos < qc0 + CHUNK)
            return jnp.where(ok, toep[:rows, :cols] * LOG2E, NEG_INF)

        for v in range(BAND_VARIANTS):
            q0 = BAND_TQ * v
            bp_ref[0, v, h2 * BAND_TQ:(h2 + 1) * BAND_TQ, :] = masked(q0, q0 - A_CACHE_ROWS, BAND_TQ, BAND_TK)
        bs_ref[0, h2 * ts:(h2 + 1) * ts, :] = masked(past_len, past_len - A_CACHE_ROWS, ts, nks)


def _bias_call(rel_bias, past_len, ts):
    rb = jnp.pad(rel_bias.reshape(N_PAIRS, 2, N_REL), ((0, 0), (0, 6), (0, N_REL_PAD - N_REL)))
    nks = A_CACHE_ROWS + ts
    return pl.pallas_call(
        functools.partial(_bias_kernel, past_len=past_len, ts=ts),
        grid=(N_PAIRS,),
        in_specs=[pl.BlockSpec((1, 8, N_REL_PAD), lambda p: (p, 0, 0))],
        out_specs=[pl.BlockSpec((1, BAND_VARIANTS, 2 * BAND_TQ, BAND_TK), lambda p: (p, 0, 0, 0)),
                   pl.BlockSpec((1, 2 * ts, nks), lambda p: (p, 0, 0))],
        out_shape=[jax.ShapeDtypeStruct((N_PAIRS, BAND_VARIANTS, 2 * BAND_TQ, BAND_TK), F32),
                   jax.ShapeDtypeStruct((N_PAIRS, 2 * ts, nks), F32)],
        name="band_bias", compiler_params=_params(("arbitrary",)))(rb)


def _first_head_lanes(rows):
    return lax.broadcasted_iota(jnp.int32, (rows, PAIR_W), 1) < HEAD_DIM


def _stack_heads(q, first):
    zero = jnp.zeros_like(q)
    return jnp.concatenate([jnp.where(first, q, zero), jnp.where(first, zero, q)], axis=0)


def _unstack_heads(o2, first):
    rows = o2.shape[0] // 2
    return jnp.where(first, o2[:rows], o2[rows:])


def _nt_dot(a, b):
    return lax.dot_general(a, b, (((1,), (1,)), ((), ())), preferred_element_type=F32)


def _gate(o, g):
    g = g.astype(F32)
    return (o * (g * jax.nn.sigmoid(g))).astype(BF16)


def _nn_dot(a, b):
    return jnp.dot(a, b, preferred_element_type=F32)


def _skewed(n_chains, n_stages, stage):
    for slot in range(n_stages + n_chains - 1):
        for c in range(n_chains):
            if 0 <= slot - c < n_stages:
                stage(c, slot - c)


def _softmax_chains(score_fns, pv_fns):
    n = len(score_fns)
    st = [dict() for _ in range(n)]

    def stage(c, s):
        d = st[c]
        if s == 0:
            d["s"] = score_fns[c]()
        elif s == 1:
            d["m"] = jnp.max(d["s"], axis=-1, keepdims=True)
        elif s == 2:
            p = jnp.exp2(d.pop("s") - d.pop("m"))
            d["l"] = jnp.sum(p, axis=-1, keepdims=True)
            d["p"] = p.astype(BF16)
        elif s == 3:
            d["o"] = pv_fns[c](d.pop("p"))
        else:
            d["o"] = d["o"] * (1.0 / d.pop("l"))

    _skewed(n, 5, stage)
    return [d["o"] for d in st]


def _band_prompt_kernel(q_ref, k_ref, v_ref, g_ref, bias_ref, o_ref, kpad, vpad, *, t):
    zeros = jnp.zeros((A_CACHE_ROWS, PAIR_W), BF16)
    kpad[0:A_CACHE_ROWS, :] = zeros
    vpad[0:A_CACHE_ROWS, :] = zeros
    kpad[A_CACHE_ROWS:, :] = k_ref[0, 0]
    vpad[A_CACHE_ROWS:, :] = v_ref[0, 0]
    first = _first_head_lanes(BAND_TQ)

    def body(i, carry):
        qts = [i * BAND_CHAINS + c for c in range(BAND_CHAINS)]
        r0s = [pl.multiple_of(qt * BAND_TQ, BAND_TQ) for qt in qts]
        def score_fn(qt, r0):
            def fn():
                q2 = _stack_heads(q_ref[0, 0, pl.ds(r0, BAND_TQ), :], first)
                kw = kpad[pl.ds(r0, BAND_TK), :]
                return _nt_dot(q2, kw) + bias_ref[0, jnp.minimum(qt, BAND_VARIANTS - 1)]
            return fn

        outs = _softmax_chains(
            [score_fn(qt, r0) for qt, r0 in zip(qts, r0s)],
            [lambda p, r0=r0: _nn_dot(p, vpad[pl.ds(r0, BAND_TK), :]) for r0 in r0s])
        for r0, o2 in zip(r0s, outs):
            o = _unstack_heads(o2, first)
            o_ref[0, 0, pl.ds(r0, BAND_TQ), :] = _gate(o, g_ref[0, 0, pl.ds(r0, BAND_TQ), :])
        return carry

    assert (t // BAND_TQ) % BAND_CHAINS == 0
    lax.fori_loop(0, t // BAND_TQ // BAND_CHAINS, body, 0)


def _band_prompt_call(q, k, v, g, bias_p):
    b, _, t, _ = q.shape
    blk = pl.BlockSpec((1, 1, t, PAIR_W), lambda p, bb: (bb, p, 0, 0))
    return pl.pallas_call(
        functools.partial(_band_prompt_kernel, t=t),
        grid=(N_PAIRS, b),
        in_specs=[blk, blk, blk, blk,
                  pl.BlockSpec((1, BAND_VARIANTS, 2 * BAND_TQ, BAND_TK), lambda p, bb: (p, 0, 0, 0))],
        out_specs=blk,
        out_shape=jax.ShapeDtypeStruct(q.shape, BF16),
        scratch_shapes=[pltpu.VMEM((t + A_CACHE_ROWS, PAIR_W), BF16),
                        pltpu.VMEM((t + A_CACHE_ROWS, PAIR_W), BF16)],
        name="band_prompt", compiler_params=_params(("arbitrary", "arbitrary")))(q, k, v, g, bias_p)


def _band_sample_kernel(q_ref, k_ref, v_ref, g_ref, ckt_ref, cvt_ref, bias_ref, o_ref, *, ts):
    first = _first_head_lanes(ts)
    units = [(i, p) for i in range(BAND_SAMPLE_BATCH) for p in range(N_PAIRS)]
    rows = lambda p: slice(p * PAIR_W, (p + 1) * PAIR_W)
    new = lambda i: slice(i * ts, (i + 1) * ts)

    def score_fn(i, p):
        def fn():
            q2 = _stack_heads(q_ref[0, p, new(i)], first)
            s_old = _nn_dot(q2, ckt_ref[i, rows(p), :].astype(BF16))
            s_new = _nt_dot(q2, k_ref[0, p, new(i)])
            return jnp.concatenate([s_old, s_new], axis=1) + bias_ref[p]
        return fn

    def pv_fn(i, p):
        def fn(prob):
            return (_nt_dot(prob[:, :A_CACHE_ROWS], cvt_ref[i, rows(p), :].astype(BF16))
                    + _nn_dot(prob[:, A_CACHE_ROWS:], v_ref[0, p, new(i)]))
        return fn

    outs = _softmax_chains([score_fn(i, p) for i, p in units], [pv_fn(i, p) for i, p in units])
    for (i, p), o2 in zip(units, outs):
        o_ref[0, p, new(i)] = _gate(_unstack_heads(o2, first), g_ref[0, p, new(i)])


def _band_sample_call(q, k, v, g, cache_kt, cache_vt, bias_s, bs, ts):
    nb = BAND_SAMPLE_BATCH
    assert bs % nb == 0
    blk = pl.BlockSpec((1, N_PAIRS, nb * ts, PAIR_W), lambda b: (0, 0, b, 0))
    cblk = pl.BlockSpec((nb, D_MODEL, A_CACHE_ROWS), lambda b: (b, 0, 0))
    return pl.pallas_call(
        functools.partial(_band_sample_kernel, ts=ts),
        grid=(bs // nb,),
        in_specs=[blk, blk, blk, blk, cblk, cblk,
                  pl.BlockSpec(bias_s.shape, lambda b: (0, 0, 0))],
        out_specs=blk,
        out_shape=jax.ShapeDtypeStruct(q.shape, BF16),
        name="band_sample", compiler_params=_params(("arbitrary",)))(q, k, v, g, cache_kt, cache_vt, bias_s)


def _suffix_matrix(n):
    r = lax.broadcasted_iota(jnp.int32, (n, n), 0)
    c = lax.broadcasted_iota(jnp.int32, (n, n), 1)
    return jnp.where(r > c, 1.0, 0.0).astype(BF16)


class SbChain(NamedTuple):
    score_fn: object
    pv_fn: object
    suffix: jax.Array
    causal: Optional[jax.Array] = None
    prev: Optional[int] = None
    carry: Optional[jax.Array] = None
    acc: Optional[jax.Array] = None


def _sb_tiles(chains):
    st = [dict() for _ in chains]

    def stage(c, s):
        ch, d = chains[c], st[c]
        if s == 0:
            d["z"] = ch.score_fn()
        elif s == 1:
            z = d.pop("z")
            sp = jnp.maximum(jnp.log2(1.0 + jnp.exp2(jnp.minimum(z, SB_Z_CLAMP))), z)
            if ch.causal is not None:
                sp = jnp.where(ch.causal, sp, 0.0)
            total = jnp.sum(sp, axis=-1, keepdims=True)
            carry = ch.carry if ch.prev is None else st[ch.prev]["carry"]
            d["sp"] = sp.astype(BF16)
            d["zs"] = z - sp if carry is None else (z - sp) - carry
            d["carry"] = total if carry is None else carry + total
        elif s == 2:
            d["later"] = _nn_dot(d.pop("sp"), ch.suffix)
        elif s == 3:
            w = jnp.exp2(d.pop("zs") - d.pop("later"))
            if ch.causal is not None:
                w = jnp.where(ch.causal, w, 0.0)
            d["w"] = w.astype(BF16)
        else:
            pv = ch.pv_fn(d.pop("w"))
            acc = ch.acc if ch.prev is None else st[ch.prev]["acc"]
            d["acc"] = pv if acc is None else acc + pv

    _skewed(len(chains), 5, stage)
    return [(d["carry"], d["acc"]) for d in st]


def _sb_prompt_kernel(q_ref, k_ref, v_ref, g_ref, o_ref, *, t):
    first = _first_head_lanes(SB_TQ)
    suffix = _suffix_matrix(SB_TK)
    row = lax.broadcasted_iota(jnp.int32, (SB_TQ, SB_TK), 0)
    col = lax.broadcasted_iota(jnp.int32, (SB_TQ, SB_TK), 1)
    causal = col < row

    pairs = range(SB_CHAINS)
    chain_pair = [p for p in pairs for _ in range(2)]

    n = len(chain_pair)

    def q_body(qt, c0):
        r0 = pl.multiple_of(qt * SB_TQ, SB_TQ)
        qh = []
        for p in pairs:
            q = q_ref[0, p, pl.ds(r0, SB_TQ), :]
            zero = jnp.zeros_like(q)
            qh += [jnp.where(first, q, zero), jnp.where(first, zero, q)]

        def tile(k0, mask=None, state=None, prev0=None):
            return [SbChain(lambda c=c, p=p: _nt_dot(qh[c], k_ref[0, p, pl.ds(k0, SB_TK), :]),
                            lambda w, p=p: _nn_dot(w, v_ref[0, p, pl.ds(k0, SB_TK), :]),
                            suffix, mask, None if prev0 is None else prev0 + c,
                            None if state is None else state[2 * c],
                            None if state is None else state[2 * c + 1])
                    for c, p in enumerate(chain_pair)]

        def flat(results):
            return tuple(x for r in results for x in r)

        def own_tile_alone():
            return flat(_sb_tiles(tile(r0, causal)))

        def own_tile_and_next():
            k1 = pl.multiple_of(r0 - SB_TK, SB_TK)
            return flat(_sb_tiles(tile(r0, causal) + tile(k1, prev0=0))[n:])

        odd = jnp.bitwise_and(qt, 1)
        state = lax.cond(odd == 1, own_tile_and_next, own_tile_alone)

        def k_body(i, st):
            ka = pl.multiple_of((qt - 1 - odd - 2 * i) * SB_TK, SB_TK)
            kb = pl.multiple_of(ka - SB_TK, SB_TK)
            return flat(_sb_tiles(tile(ka, state=st) + tile(kb, prev0=0))[n:])

        state = lax.fori_loop(0, lax.shift_right_logical(qt, 1), k_body, state)
        for p in pairs:
            o = jnp.where(first, state[4 * p + 1], state[4 * p + 3])
            o_ref[0, p, pl.ds(r0, SB_TQ), :] = _gate(o, g_ref[0, p, pl.ds(r0, SB_TQ), :])
        return c0

    lax.fori_loop(0, t // SB_TQ, q_body, 0)


def _sb_prompt_call(q, k, v, g):
    b, _, t, _ = q.shape
    blk = pl.BlockSpec((1, SB_CHAINS, t, PAIR_W), lambda bb, p: (bb, p, 0, 0))
    return pl.pallas_call(
        functools.partial(_sb_prompt_kernel, t=t),
        grid=(b, N_PAIRS // SB_CHAINS),
        in_specs=[blk, blk, blk, blk],
        out_specs=blk,
        out_shape=jax.ShapeDtypeStruct(q.shape, BF16),
        name="sb_prompt", compiler_params=_params(("arbitrary", "arbitrary")))(q, k, v, g)


def _sb_sample_kernel(q_ref, k_ref, v_ref, g_ref, ckt_ref, cvt_ref, o_ref, carry_ref, acc_ref, *, ts, n_steps):
    step = pl.program_id(1)
    first = _first_head_lanes(ts)
    pairs = range(N_PAIRS)
    rows = [slice(p * PAIR_W, (p + 1) * PAIR_W) for p in pairs]
    q2 = [_stack_heads(q_ref[0, p], first) for p in pairs]
    suffix = _suffix_matrix(SB_TK)

    def cached_tiles(chains, state=None):
        for i in reversed(range(SB_SAMPLE_KEYS // SB_TK)):
            keys = slice(i * SB_TK, (i + 1) * SB_TK)
            base = len(chains) - N_PAIRS
            chains = chains + [
                SbChain(lambda p=p, keys=keys: _nn_dot(q2[p], ckt_ref[0, rows[p], keys].astype(BF16)),
                        lambda w, p=p, keys=keys: _nt_dot(w, cvt_ref[0, rows[p], keys].astype(BF16)),
                        suffix, None, base + p if base >= 0 else None,
                        None if base >= 0 else state[p][0], None if base >= 0 else state[p][1])
                for p in pairs]
        return chains

    def run(chains):
        out = _sb_tiles(chains)[-N_PAIRS:]
        for p in pairs:
            carry_ref[p], acc_ref[p] = out[p]

    @pl.when(step == 0)
    def _():
        row = lax.broadcasted_iota(jnp.int32, (2 * ts, ts), 0)
        col = lax.broadcasted_iota(jnp.int32, (2 * ts, ts), 1)
        causal = col < jnp.bitwise_and(row, ts - 1)
        new_rows = [SbChain(lambda p=p: _nt_dot(q2[p], k_ref[0, p]), lambda w, p=p: _nn_dot(w, v_ref[0, p]),
                            _suffix_matrix(ts), causal) for p in pairs]
        run(cached_tiles(new_rows))

    @pl.when(step > 0)
    def _():
        run(cached_tiles([], [(carry_ref[p], acc_ref[p]) for p in pairs]))

    @pl.when(step == n_steps - 1)
    def _():
        for p in range(N_PAIRS):
            o_ref[0, p] = _gate(_unstack_heads(acc_ref[p], first), g_ref[0, p])


def _sb_sample_call(q, k, v, g, cache_kt, cache_vt, bs, ts):
    past = cache_kt.shape[2]
    assert past % SB_SAMPLE_KEYS == 0
    n_steps = past // SB_SAMPLE_KEYS
    blk = pl.BlockSpec((1, N_PAIRS, ts, PAIR_W), lambda b, s: (0, 0, b, 0))
    cblk = pl.BlockSpec((1, D_MODEL, SB_SAMPLE_KEYS), lambda b, s: (b, 0, n_steps - 1 - s))
    return pl.pallas_call(
        functools.partial(_sb_sample_kernel, ts=ts, n_steps=n_steps),
        grid=(bs, n_steps),
        in_specs=[blk, blk, blk, blk, cblk, cblk],
        out_specs=blk,
        out_shape=jax.ShapeDtypeStruct(q.shape, BF16),
        scratch_shapes=[pltpu.VMEM((N_PAIRS, 2 * ts, 1), F32),
                        pltpu.VMEM((N_PAIRS, 2 * ts, PAIR_W), F32)],
        name="sb_sample", compiler_params=_params(("arbitrary", "arbitrary")))(q, k, v, g, cache_kt, cache_vt)


def kernel(x_prompt, x_sample, cache_a_k, cache_a_v, cache_b_k, cache_b_v, norm_a, w_in_a, rel_bias_a,
           w_out_a, norm_kv, w_kv, norm_b, w_in_b, w_out_b, norm_f):
    b, t, d = x_prompt.shape
    bs, ts, _ = x_sample.shape
    past = cache_b_k.shape[1]
    assert d == D_MODEL and norm_a.shape[0] == 1 and norm_b.shape[0] == 1
    assert cache_a_k.shape[2] == A_CACHE_ROWS and t % SB_TQ == 0 and t >= A_CACHE_ROWS
    assert past % CHUNK == 0 and ts == CHUNK

    w_a = w_in_a[0].astype(BF16)
    wo_a = w_out_a[0].astype(BF16)
    w_kvb = w_kv.astype(BF16)
    w_b = w_in_b[0].astype(BF16)
    wo_b = w_out_b[0].astype(BF16)
    g_a = norm_a
    g_b = jnp.stack([norm_kv, norm_b[0]])
    g_f = norm_f[None]
    xs = x_sample.reshape(1, bs * ts, d)

    time_minor = lambda c: jnp.transpose(c, (0, 2, 3, 1)).reshape(c.shape[0], d, c.shape[1])

    bias_p, bias_s = _bias_call(rel_bias_a[0], past, ts)

    plan_a = (Seg(0, 0, 0, SCALE * LOG2E, 0, None), Seg(0, 0, 1, 1.0, 1, 0),
              Seg(0, 0, 2, 1.0, 2, 1), Seg(0, 0, 3, 1.0, 3, None))
    plan_b = (Seg(0, 0, 0, 1.0, 0, 0), Seg(0, 0, 1, 1.0, 1, 1),
              Seg(1, 1, 0, SCALE * LOG2E, 2, None), Seg(1, 1, 1, 1.0, 3, None))

    qp, kp, vp, gp, akp, avp = _dense_call(x_prompt, gains=g_a, weights=(w_a,), plan=plan_a,
                                           f32_last_rows=A_CACHE_ROWS, f32_layout="time_minor",
                                           name="proj_a_prompt")
    qs, ks, vs, gs, aks, avs = _dense_call(xs, gains=g_a, weights=(w_a,), plan=plan_a, f32_layout="heads",
                                           name="proj_a_sample")

    ogp = _band_prompt_call(qp, kp, vp, gp, bias_p)
    ogs = _band_sample_call(qs, ks, vs, gs, time_minor(cache_a_k[0]), time_minor(cache_a_v[0]),
                            bias_s, bs, ts)

    xp1, kbp, vbp, qbp, gbp, kbp32, vbp32 = _dense_call(
        x_prompt, og=ogp, wo=wo_a, gains=g_b, weights=(w_kvb, w_b), plan=plan_b, emit_x=True,
        f32_layout="time_minor", name="out_a_proj_b_prompt")
    xs1, kbs, vbs, qbs, gbs, kbs32, vbs32 = _dense_call(
        xs, og=ogs, wo=wo_a, gains=g_b, weights=(w_kvb, w_b), plan=plan_b, emit_x=True, f32_layout="heads",
        name="out_a_proj_b_sample")

    obp = _sb_prompt_call(qbp, kbp, vbp, gbp)
    obs = _sb_sample_call(qbs, kbs, vbs, gbs, time_minor(cache_b_k), time_minor(cache_b_v), bs, ts)

    (y_prompt,) = _dense_call(xp1, og=obp, wo=wo_b, gf=g_f, name="out_b_prompt")
    (y_sample,) = _dense_call(xs1, og=obs, wo=wo_b, gf=g_f, name="out_b_sample")

    heads = lambda a, n, rows: a.reshape(n, rows, N_HEADS, HEAD_DIM)
    from_time_minor = lambda a: jnp.transpose(a.reshape(a.shape[0], N_HEADS, HEAD_DIM, a.shape[2]), (0, 3, 1, 2))
    return (y_prompt, y_sample.reshape(bs, ts, d),
            from_time_minor(akp)[None], from_time_minor(avp)[None],
            from_time_minor(kbp32), from_time_minor(vbp32),
            heads(aks, bs, ts)[None], heads(avs, bs, ts)[None],
            heads(kbs32, bs, ts), heads(vbs32, bs, ts))
```

```python
import functools
from typing import NamedTuple, Optional

import jax
import jax.numpy as jnp
from jax import lax
from jax.experimental import pallas as pl
from jax.experimental.pallas import tpu as pltpu

D_MODEL = 1024
N_HEADS = 16
HEAD_DIM = 64
PAIR_W = 2 * HEAD_DIM
N_PAIRS = N_HEADS // 2
CHUNK = 64
LEFT_CHUNKS = 8
A_CACHE_ROWS = LEFT_CHUNKS * CHUNK
REL_CLIP = 128
N_REL = 2 * REL_CLIP + 1
N_REL_PAD = 384
RMS_EPS = 1e-6
NEG_INF = -1e30
SCALE = HEAD_DIM ** -0.5
LOG2E = 1.4426950408889634

BAND_TQ = 128
BAND_TK = A_CACHE_ROWS + BAND_TQ
BAND_VARIANTS = A_CACHE_ROWS // BAND_TQ + 1
BAND_SAMPLE_BATCH = 2
BAND_CHAINS = 16
SB_TQ = 256
SB_TK = 256
SB_CHAINS = 4
SB_SAMPLE_KEYS = 2048
SB_Z_CLAMP = 126.0

VMEM_LIMIT = 56 * 1024 * 1024

F32 = jnp.float32
BF16 = jnp.bfloat16


def _params(sem):
    return pltpu.CompilerParams(dimension_semantics=sem, vmem_limit_bytes=VMEM_LIMIT)


class Seg(NamedTuple):
    norm: int
    w: int
    col: int
    scale: float
    pm: Optional[int]
    f32: Optional[int]


def _dense_kernel(*refs, has_resid, emit_x, n_norm, n_w, plan, final_norm, n_pm, n_f32, f32_layout):
    it = iter(refs)
    x_ref = next(it)
    og_ref = next(it) if has_resid else None
    wo_ref = next(it) if has_resid else None
    g_ref = next(it) if n_norm else None
    w_refs = [next(it) for _ in range(n_w)]
    gf_ref = next(it) if final_norm else None
    xo_ref = next(it) if emit_x else None
    pm_refs = [next(it) for _ in range(n_pm)]
    f32_refs = [next(it) for _ in range(n_f32)]
    y_ref = next(it) if final_norm else None

    x = x_ref[0]
    if has_resid:
        og = jnp.concatenate([og_ref[0, p] for p in range(N_PAIRS)], axis=1)
        x = x + jnp.dot(og, wo_ref[...], preferred_element_type=F32)
        if emit_x:
            xo_ref[0] = x
    if n_norm or final_norm:
        xn = x * lax.rsqrt(jnp.mean(x * x, axis=-1, keepdims=True) + RMS_EPS)
    if final_norm:
        y_ref[0] = xn * gf_ref[...]
    hs = [(xn * g_ref[i:i + 1, :]).astype(BF16) for i in range(n_norm)]
    for seg in plan:
        w = w_refs[seg.w][:, seg.col * D_MODEL:(seg.col + 1) * D_MODEL]
        acc = jnp.dot(hs[seg.norm], w, preferred_element_type=F32)
        if seg.f32 is not None:
            if f32_layout == "heads":
                f32_refs[seg.f32][0] = pltpu.einshape("r(hd)->rhd", acc, h=N_HEADS)
            elif f32_layout == "time_minor":
                f32_refs[seg.f32][0] = acc.T
            else:
                f32_refs[seg.f32][0] = acc
        if seg.pm is not None:
            ab = (acc * seg.scale).astype(BF16) if seg.scale != 1.0 else acc.astype(BF16)
            for p in range(N_PAIRS):
                pm_refs[seg.pm][0, p] = ab[:, p * PAIR_W:(p + 1) * PAIR_W]


def _dense_call(x, *, og=None, wo=None, gains=None, weights=(), plan=(), gf=None,
                emit_x=False, f32_last_rows=None, f32_layout="rows", tm=512, name="dense"):
    bx, tx, d = x.shape
    assert d == D_MODEL and tx % tm == 0
    has_resid = og is not None
    n_norm = 0 if gains is None else gains.shape[0]
    n_pm = sum(s.pm is not None for s in plan)
    n_f32 = sum(s.f32 is not None for s in plan)
    final_norm = gf is not None

    row_spec = pl.BlockSpec((1, tm, d), lambda b, t: (b, t, 0))
    pm_spec = pl.BlockSpec((1, N_PAIRS, tm, PAIR_W), lambda b, t: (b, 0, t, 0))
    whole = lambda a: pl.BlockSpec(a.shape, lambda b, t: (0,) * a.ndim, pipeline_mode=pl.Buffered(1))

    in_arrays, in_specs = [x], [row_spec]
    if has_resid:
        in_arrays += [og, wo]
        in_specs += [pm_spec, whole(wo)]
    if n_norm:
        in_arrays.append(gains)
        in_specs.append(whole(gains))
    for w in weights:
        in_arrays.append(w)
        in_specs.append(whole(w))
    if final_norm:
        in_arrays.append(gf)
        in_specs.append(whole(gf))

    out_shapes, out_specs = [], []
    if emit_x:
        out_shapes.append(jax.ShapeDtypeStruct((bx, tx, d), F32))
        out_specs.append(row_spec)
    for _ in range(n_pm):
        out_shapes.append(jax.ShapeDtypeStruct((bx, N_PAIRS, tx, PAIR_W), BF16))
        out_specs.append(pm_spec)
    assert f32_last_rows in (None, tm)
    rows_kept = tx if f32_last_rows is None else tm
    keep = (lambda t: t) if f32_last_rows is None else (lambda t: 0)
    for _ in range(n_f32):
        if f32_layout == "heads":
            out_shapes.append(jax.ShapeDtypeStruct((bx, rows_kept, N_HEADS, HEAD_DIM), F32))
            out_specs.append(pl.BlockSpec((1, tm, N_HEADS, HEAD_DIM), lambda b, t: (b, keep(t), 0, 0)))
        elif f32_layout == "time_minor":
            out_shapes.append(jax.ShapeDtypeStruct((bx, d, rows_kept), F32))
            out_specs.append(pl.BlockSpec((1, d, tm), lambda b, t: (b, 0, keep(t))))
        else:
            out_shapes.append(jax.ShapeDtypeStruct((bx, rows_kept, d), F32))
            out_specs.append(pl.BlockSpec((1, tm, d), lambda b, t: (b, keep(t), 0)))
    if final_norm:
        out_shapes.append(jax.ShapeDtypeStruct((bx, tx, d), F32))
        out_specs.append(row_spec)

    body = functools.partial(
        _dense_kernel, has_resid=has_resid, emit_x=emit_x, n_norm=n_norm, n_w=len(weights),
        plan=tuple(plan), final_norm=final_norm, n_pm=n_pm, n_f32=n_f32, f32_layout=f32_layout)
    return pl.pallas_call(
        body, grid=(bx, tx // tm), in_specs=in_specs, out_specs=out_specs, out_shape=out_shapes,
        name=name, compiler_params=_params(("arbitrary", "arbitrary")))(*in_arrays)


def _bias_kernel(rb_ref, bp_ref, bs_ref, *, past_len, ts):
    width = 768
    d0 = A_CACHE_ROWS
    rb = rb_ref[0]
    hi = rb.astype(BF16)
    r1 = rb - hi.astype(F32)
    mid = r1.astype(BF16)
    lo = (r1 - mid.astype(F32)).astype(BF16)
    c = lax.broadcasted_iota(jnp.int32, (N_REL_PAD, width), 0)
    n = lax.broadcasted_iota(jnp.int32, (N_REL_PAD, width), 1)
    m = jnp.where(n < BAND_TK, n, n - width)
    tgt = jnp.clip(d0 - m, -REL_CLIP, REL_CLIP) + REL_CLIP
    onehot = jnp.where(c == tgt, 1.0, 0.0).astype(BF16)
    r_ext = (jnp.dot(hi, onehot, preferred_element_type=F32)
             + jnp.dot(mid, onehot, preferred_element_type=F32)
             + jnp.dot(lo, onehot, preferred_element_type=F32))

    sub = lax.broadcasted_iota(jnp.int32, (8, width), 0)
    nks = A_CACHE_ROWS + ts
    for h2 in range(2):
        base = jnp.broadcast_to(r_ext[h2:h2 + 1, :], (8, width))
        b8 = base
        for r in range(1, 8):
            b8 = jnp.where(sub == r, pltpu.roll(base, r, 1), b8)
        blocks = [b8] + [pltpu.roll(b8, 8 * gi, 1) for gi in range(1, BAND_TQ // 8)]
        toep = jnp.concatenate(blocks, axis=0)[:, :BAND_TK]

        def masked(q0, k0, rows, cols):
            qpos = q0 + lax.broadcasted_iota(jnp.int32, (rows, cols), 0)
            kpos = k0 + lax.broadcasted_iota(jnp.int32, (rows, cols), 1)
            qc0 = qpos - jnp.bitwise_and(qpos, CHUNK - 1)
            lo_k = jnp.maximum(qc0 - A_CACHE_ROWS, 0)
            ok = jnp.logical_and(kpos >= lo_k, kpos < qc0 + CHUNK)
            return jnp.where(ok, toep[:rows, :cols] * LOG2E, NEG_INF)

        for v in range(BAND_VARIANTS):
            q0 = BAND_TQ * v
            bp_ref[0, v, h2 * BAND_TQ:(h2 + 1) * BAND_TQ, :] = masked(q0, q0 - A_CACHE_ROWS, BAND_TQ, BAND_TK)
        bs_ref[0, h2 * ts:(h2 + 1) * ts, :] = masked(past_len, past_len - A_CACHE_ROWS, ts, nks)


def _bias_call(rel_bias, past_len, ts):
    rb = jnp.pad(rel_bias.reshape(N_PAIRS, 2, N_REL), ((0, 0), (0, 6), (0, N_REL_PAD - N_REL)))
    nks = A_CACHE_ROWS + ts
    return pl.pallas_call(
        functools.partial(_bias_kernel, past_len=past_len, ts=ts),
        grid=(N_PAIRS,),
        in_specs=[pl.BlockSpec((1, 8, N_REL_PAD), lambda p: (p, 0, 0))],
        out_specs=[pl.BlockSpec((1, BAND_VARIANTS, 2 * BAND_TQ, BAND_TK), lambda p: (p, 0, 0, 0)),
                   pl.BlockSpec((1, 2 * ts, nks), lambda p: (p, 0, 0))],
        out_shape=[jax.ShapeDtypeStruct((N_PAIRS, BAND_VARIANTS, 2 * BAND_TQ, BAND_TK), F32),
                   jax.ShapeDtypeStruct((N_PAIRS, 2 * ts, nks), F32)],
        name="band_bias", compiler_params=_params(("arbitrary",)))(rb)


def _first_head_lanes(rows):
    return lax.broadcasted_iota(jnp.int32, (rows, PAIR_W), 1) < HEAD_DIM


def _stack_heads(q, first):
    zero = jnp.zeros_like(q)
    return jnp.concatenate([jnp.where(first, q, zero), jnp.where(first, zero, q)], axis=0)


def _unstack_heads(o2, first):
    rows = o2.shape[0] // 2
    return jnp.where(first, o2[:rows], o2[rows:])


def _nt_dot(a, b):
    return lax.dot_general(a, b, (((1,), (1,)), ((), ())), preferred_element_type=F32)


def _gate(o, g):
    g = g.astype(F32)
    return (o * (g * jax.nn.sigmoid(g))).astype(BF16)


def _nn_dot(a, b):
    return jnp.dot(a, b, preferred_element_type=F32)


def _skewed(n_chains, n_stages, stage):
    for slot in range(n_stages + n_chains - 1):
        for c in range(n_chains):
            if 0 <= slot - c < n_stages:
                stage(c, slot - c)


def _softmax_chains(score_fns, pv_fns):
    n = len(score_fns)
    st = [dict() for _ in range(n)]

    def stage(c, s):
        d = st[c]
        if s == 0:
            d["s"] = score_fns[c]()
        elif s == 1:
            d["m"] = jnp.max(d["s"], axis=-1, keepdims=True)
        elif s == 2:
            p = jnp.exp2(d.pop("s") - d.pop("m"))
            d["l"] = jnp.sum(p, axis=-1, keepdims=True)
            d["p"] = p.astype(BF16)
        elif s == 3:
            d["o"] = pv_fns[c](d.pop("p"))
        else:
            d["o"] = d["o"] * (1.0 / d.pop("l"))

    _skewed(n, 5, stage)
    return [d["o"] for d in st]


def _band_prompt_kernel(q_ref, k_ref, v_ref, g_ref, bias_ref, o_ref, kpad, vpad, *, t):
    zeros = jnp.zeros((A_CACHE_ROWS, PAIR_W), BF16)
    kpad[0:A_CACHE_ROWS, :] = zeros
    vpad[0:A_CACHE_ROWS, :] = zeros
    kpad[A_CACHE_ROWS:, :] = k_ref[0, 0]
    vpad[A_CACHE_ROWS:, :] = v_ref[0, 0]
    first = _first_head_lanes(BAND_TQ)

    def body(i, carry):
        qts = [i * BAND_CHAINS + c for c in range(BAND_CHAINS)]
        r0s = [pl.multiple_of(qt * BAND_TQ, BAND_TQ) for qt in qts]
        def score_fn(qt, r0):
            def fn():
                q2 = _stack_heads(q_ref[0, 0, pl.ds(r0, BAND_TQ), :], first)
                kw = kpad[pl.ds(r0, BAND_TK), :]
                return _nt_dot(q2, kw) + bias_ref[0, jnp.minimum(qt, BAND_VARIANTS - 1)]
            return fn

        outs = _softmax_chains(
            [score_fn(qt, r0) for qt, r0 in zip(qts, r0s)],
            [lambda p, r0=r0: _nn_dot(p, vpad[pl.ds(r0, BAND_TK), :]) for r0 in r0s])
        for r0, o2 in zip(r0s, outs):
            o = _unstack_heads(o2, first)
            o_ref[0, 0, pl.ds(r0, BAND_TQ), :] = _gate(o, g_ref[0, 0, pl.ds(r0, BAND_TQ), :])
        return carry

    assert (t // BAND_TQ) % BAND_CHAINS == 0
    lax.fori_loop(0, t // BAND_TQ // BAND_CHAINS, body, 0)


def _band_prompt_call(q, k, v, g, bias_p):
    b, _, t, _ = q.shape
    blk = pl.BlockSpec((1, 1, t, PAIR_W), lambda p, bb: (bb, p, 0, 0))
    return pl.pallas_call(
        functools.partial(_band_prompt_kernel, t=t),
        grid=(N_PAIRS, b),
        in_specs=[blk, blk, blk, blk,
                  pl.BlockSpec((1, BAND_VARIANTS, 2 * BAND_TQ, BAND_TK), lambda p, bb: (p, 0, 0, 0))],
        out_specs=blk,
        out_shape=jax.ShapeDtypeStruct(q.shape, BF16),
        scratch_shapes=[pltpu.VMEM((t + A_CACHE_ROWS, PAIR_W), BF16),
                        pltpu.VMEM((t + A_CACHE_ROWS, PAIR_W), BF16)],
        name="band_prompt", compiler_params=_params(("arbitrary", "arbitrary")))(q, k, v, g, bias_p)


def _band_sample_kernel(q_ref, k_ref, v_ref, g_ref, ckt_ref, cvt_ref, bias_ref, o_ref, *, ts):
    first = _first_head_lanes(ts)
    units = [(i, p) for i in range(BAND_SAMPLE_BATCH) for p in range(N_PAIRS)]
    rows = lambda p: slice(p * PAIR_W, (p + 1) * PAIR_W)
    new = lambda i: slice(i * ts, (i + 1) * ts)

    def score_fn(i, p):
        def fn():
            q2 = _stack_heads(q_ref[0, p, new(i)], first)
            s_old = _nn_dot(q2, ckt_ref[i, rows(p), :].astype(BF16))
            s_new = _nt_dot(q2, k_ref[0, p, new(i)])
            return jnp.concatenate([s_old, s_new], axis=1) + bias_ref[p]
        return fn

    def pv_fn(i, p):
        def fn(prob):
            return (_nt_dot(prob[:, :A_CACHE_ROWS], cvt_ref[i, rows(p), :].astype(BF16))
                    + _nn_dot(prob[:, A_CACHE_ROWS:], v_ref[0, p, new(i)]))
        return fn

    outs = _softmax_chains([score_fn(i, p) for i, p in units], [pv_fn(i, p) for i, p in units])
    for (i, p), o2 in zip(units, outs):
        o_ref[0, p, new(i)] = _gate(_unstack_heads(o2, first), g_ref[0, p, new(i)])


def _band_sample_call(q, k, v, g, cache_kt, cache_vt, bias_s, bs, ts):
    nb = BAND_SAMPLE_BATCH
    assert bs % nb == 0
    blk = pl.BlockSpec((1, N_PAIRS, nb * ts, PAIR_W), lambda b: (0, 0, b, 0))
    cblk = pl.BlockSpec((nb, D_MODEL, A_CACHE_ROWS), lambda b: (b, 0, 0))
    return pl.pallas_call(
        functools.partial(_band_sample_kernel, ts=ts),
        grid=(bs // nb,),
        in_specs=[blk, blk, blk, blk, cblk, cblk,
                  pl.BlockSpec(bias_s.shape, lambda b: (0, 0, 0))],
        out_specs=blk,
        out_shape=jax.ShapeDtypeStruct(q.shape, BF16),
        name="band_sample", compiler_params=_params(("arbitrary",)))(q, k, v, g, cache_kt, cache_vt, bias_s)


def _suffix_matrix(n):
    r = lax.broadcasted_iota(jnp.int32, (n, n), 0)
    c = lax.broadcasted_iota(jnp.int32, (n, n), 1)
    return jnp.where(r > c, 1.0, 0.0).astype(BF16)


class SbChain(NamedTuple):
    score_fn: object
    pv_fn: object
    suffix: jax.Array
    causal: Optional[jax.Array] = None
    prev: Optional[int] = None
    carry: Optional[jax.Array] = None
    acc: Optional[jax.Array] = None


def _sb_tiles(chains):
    st = [dict() for _ in chains]

    def stage(c, s):
        ch, d = chains[c], st[c]
        if s == 0:
            d["z"] = ch.score_fn()
        elif s == 1:
            z = d.pop("z")
            sp = jnp.maximum(jnp.log2(1.0 + jnp.exp2(jnp.minimum(z, SB_Z_CLAMP))), z)
            if ch.causal is not None:
                sp = jnp.where(ch.causal, sp, 0.0)
            total = jnp.sum(sp, axis=-1, keepdims=True)
            carry = ch.carry if ch.prev is None else st[ch.prev]["carry"]
            d["sp"] = sp.astype(BF16)
            d["zs"] = z - sp if carry is None else (z - sp) - carry
            d["carry"] = total if carry is None else carry + total
        elif s == 2:
            d["later"] = _nn_dot(d.pop("sp"), ch.suffix)
        elif s == 3:
            w = jnp.exp2(d.pop("zs") - d.pop("later"))
            if ch.causal is not None:
                w = jnp.where(ch.causal, w, 0.0)
            d["w"] = w.astype(BF16)
        else:
            pv = ch.pv_fn(d.pop("w"))
            acc = ch.acc if ch.prev is None else st[ch.prev]["acc"]
            d["acc"] = pv if acc is None else acc + pv

    _skewed(len(chains), 5, stage)
    return [(d["carry"], d["acc"]) for d in st]


def _sb_prompt_kernel(q_ref, k_ref, v_ref, g_ref, o_ref, *, t):
    first = _first_head_lanes(SB_TQ)
    suffix = _suffix_matrix(SB_TK)
    row = lax.broadcasted_iota(jnp.int32, (SB_TQ, SB_TK), 0)
    col = lax.broadcasted_iota(jnp.int32, (SB_TQ, SB_TK), 1)
    causal = col < row

    pairs = range(SB_CHAINS)
    chain_pair = [p for p in pairs for _ in range(2)]

    n = len(chain_pair)

    def q_body(qt, c0):
        r0 = pl.multiple_of(qt * SB_TQ, SB_TQ)
        qh = []
        for p in pairs:
            q = q_ref[0, p, pl.ds(r0, SB_TQ), :]
            zero = jnp.zeros_like(q)
            qh += [jnp.where(first, q, zero), jnp.where(first, zero, q)]

        def tile(k0, mask=None, state=None, prev0=None):
            return [SbChain(lambda c=c, p=p: _nt_dot(qh[c], k_ref[0, p, pl.ds(k0, SB_TK), :]),
                            lambda w, p=p: _nn_dot(w, v_ref[0, p, pl.ds(k0, SB_TK), :]),
                            suffix, mask, None if prev0 is None else prev0 + c,
                            None if state is None else state[2 * c],
                            None if state is None else state[2 * c + 1])
                    for c, p in enumerate(chain_pair)]

        def flat(results):
            return tuple(x for r in results for x in r)

        def own_tile_alone():
            return flat(_sb_tiles(tile(r0, causal)))

        def own_tile_and_next():
            k1 = pl.multiple_of(r0 - SB_TK, SB_TK)
            return flat(_sb_tiles(tile(r0, causal) + tile(k1, prev0=0))[n:])

        odd = jnp.bitwise_and(qt, 1)
        state = lax.cond(odd == 1, own_tile_and_next, own_tile_alone)

        def k_body(i, st):
            ka = pl.multiple_of((qt - 1 - odd - 2 * i) * SB_TK, SB_TK)
            kb = pl.multiple_of(ka - SB_TK, SB_TK)
            return flat(_sb_tiles(tile(ka, state=st) + tile(kb, prev0=0))[n:])

        state = lax.fori_loop(0, lax.shift_right_logical(qt, 1), k_body, state)
        for p in pairs:
            o = jnp.where(first, state[4 * p + 1], state[4 * p + 3])
            o_ref[0, p, pl.ds(r0, SB_TQ), :] = _gate(o, g_ref[0, p, pl.ds(r0, SB_TQ), :])
        return c0

    lax.fori_loop(0, t // SB_TQ, q_body, 0)


def _sb_prompt_call(q, k, v, g):
    b, _, t, _ = q.shape
    blk = pl.BlockSpec((1, SB_CHAINS, t, PAIR_W), lambda bb, p: (bb, p, 0, 0))
    return pl.pallas_call(
        functools.partial(_sb_prompt_kernel, t=t),
        grid=(b, N_PAIRS // SB_CHAINS),
        in_specs=[blk, blk, blk, blk],
        out_specs=blk,
        out_shape=jax.ShapeDtypeStruct(q.shape, BF16),
        name="sb_prompt", compiler_params=_params(("arbitrary", "arbitrary")))(q, k, v, g)


def _sb_sample_kernel(q_ref, k_ref, v_ref, g_ref, ckt_ref, cvt_ref, o_ref, carry_ref, acc_ref, *, ts, n_steps):
    step = pl.program_id(1)
    first = _first_head_lanes(ts)
    pairs = range(N_PAIRS)
    rows = [slice(p * PAIR_W, (p + 1) * PAIR_W) for p in pairs]
    q2 = [_stack_heads(q_ref[0, p], first) for p in pairs]
    suffix = _suffix_matrix(SB_TK)

    def cached_tiles(chains, state=None):
        for i in reversed(range(SB_SAMPLE_KEYS // SB_TK)):
            keys = slice(i * SB_TK, (i + 1) * SB_TK)
            base = len(chains) - N_PAIRS
            chains = chains + [
                SbChain(lambda p=p, keys=keys: _nn_dot(q2[p], ckt_ref[0, rows[p], keys].astype(BF16)),
                        lambda w, p=p, keys=keys: _nt_dot(w, cvt_ref[0, rows[p], keys].astype(BF16)),
                        suffix, None, base + p if base >= 0 else None,
                        None if base >= 0 else state[p][0], None if base >= 0 else state[p][1])
                for p in pairs]
        return chains

    def run(chains):
        out = _sb_tiles(chains)[-N_PAIRS:]
        for p in pairs:
            carry_ref[p], acc_ref[p] = out[p]

    @pl.when(step == 0)
    def _():
        row = lax.broadcasted_iota(jnp.int32, (2 * ts, ts), 0)
        col = lax.broadcasted_iota(jnp.int32, (2 * ts, ts), 1)
        causal = col < jnp.bitwise_and(row, ts - 1)
        new_rows = [SbChain(lambda p=p: _nt_dot(q2[p], k_ref[0, p]), lambda w, p=p: _nn_dot(w, v_ref[0, p]),
                            _suffix_matrix(ts), causal) for p in pairs]
        run(cached_tiles(new_rows))

    @pl.when(step > 0)
    def _():
        run(cached_tiles([], [(carry_ref[p], acc_ref[p]) for p in pairs]))

    @pl.when(step == n_steps - 1)
    def _():
        for p in range(N_PAIRS):
            o_ref[0, p] = _gate(_unstack_heads(acc_ref[p], first), g_ref[0, p])


def _sb_sample_call(q, k, v, g, cache_kt, cache_vt, bs, ts):
    past = cache_kt.shape[2]
    assert past % SB_SAMPLE_KEYS == 0
    n_steps = past // SB_SAMPLE_KEYS
    blk = pl.BlockSpec((1, N_PAIRS, ts, PAIR_W), lambda b, s: (0, 0, b, 0))
    cblk = pl.BlockSpec((1, D_MODEL, SB_SAMPLE_KEYS), lambda b, s: (b, 0, n_steps - 1 - s))
    return pl.pallas_call(
        functools.partial(_sb_sample_kernel, ts=ts, n_steps=n_steps),
        grid=(bs, n_steps),
        in_specs=[blk, blk, blk, blk, cblk, cblk],
        out_specs=blk,
        out_shape=jax.ShapeDtypeStruct(q.shape, BF16),
        scratch_shapes=[pltpu.VMEM((N_PAIRS, 2 * ts, 1), F32),
                        pltpu.VMEM((N_PAIRS, 2 * ts, PAIR_W), F32)],
        name="sb_sample", compiler_params=_params(("arbitrary", "arbitrary")))(q, k, v, g, cache_kt, cache_vt)


def kernel(x_prompt, x_sample, cache_a_k, cache_a_v, cache_b_k, cache_b_v, norm_a, w_in_a, rel_bias_a,
           w_out_a, norm_kv, w_kv, norm_b, w_in_b, w_out_b, norm_f):
    b, t, d = x_prompt.shape
    bs, ts, _ = x_sample.shape
    past = cache_b_k.shape[1]
    assert d == D_MODEL and norm_a.shape[0] == 1 and norm_b.shape[0] == 1
    assert cache_a_k.shape[2] == A_CACHE_ROWS and t % SB_TQ == 0 and t >= A_CACHE_ROWS
    assert past % CHUNK == 0 and ts == CHUNK

    w_a = w_in_a[0].astype(BF16)
    wo_a = w_out_a[0].astype(BF16)
    w_kvb = w_kv.astype(BF16)
    w_b = w_in_b[0].astype(BF16)
    wo_b = w_out_b[0].astype(BF16)
    g_a = norm_a
    g_b = jnp.stack([norm_kv, norm_b[0]])
    g_f = norm_f[None]
    xs = x_sample.reshape(1, bs * ts, d)

    time_minor = lambda c: jnp.transpose(c, (0, 2, 3, 1)).reshape(c.shape[0], d, c.shape[1])

    bias_p, bias_s = _bias_call(rel_bias_a[0], past, ts)

    plan_a = (Seg(0, 0, 0, SCALE * LOG2E, 0, None), Seg(0, 0, 1, 1.0, 1, 0),
              Seg(0, 0, 2, 1.0, 2, 1), Seg(0, 0, 3, 1.0, 3, None))
    plan_b = (Seg(0, 0, 0, 1.0, 0, 0), Seg(0, 0, 1, 1.0, 1, 1),
              Seg(1, 1, 0, SCALE * LOG2E, 2, None), Seg(1, 1, 1, 1.0, 3, None))

    qp, kp, vp, gp, akp, avp = _dense_call(x_prompt, gains=g_a, weights=(w_a,), plan=plan_a,
                                           f32_last_rows=A_CACHE_ROWS, f32_layout="time_minor",
                                           name="proj_a_prompt")
    qs, ks, vs, gs, aks, avs = _dense_call(xs, gains=g_a, weights=(w_a,), plan=plan_a, f32_layout="heads",
                                           name="proj_a_sample")

    ogp = _band_prompt_call(qp, kp, vp, gp, bias_p)
    ogs = _band_sample_call(qs, ks, vs, gs, time_minor(cache_a_k[0]), time_minor(cache_a_v[0]),
                            bias_s, bs, ts)

    xp1, kbp, vbp, qbp, gbp, kbp32, vbp32 = _dense_call(
        x_prompt, og=ogp, wo=wo_a, gains=g_b, weights=(w_kvb, w_b), plan=plan_b, emit_x=True,
        f32_layout="time_minor", name="out_a_proj_b_prompt")
    xs1, kbs, vbs, qbs, gbs, kbs32, vbs32 = _dense_call(
        xs, og=ogs, wo=wo_a, gains=g_b, weights=(w_kvb, w_b), plan=plan_b, emit_x=True, f32_layout="heads",
        name="out_a_proj_b_sample")

    obp = _sb_prompt_call(qbp, kbp, vbp, gbp)
    obs = _sb_sample_call(qbs, kbs, vbs, gbs, time_minor(cache_b_k), time_minor(cache_b_v), bs, ts)

    (y_prompt,) = _dense_call(xp1, og=obp, wo=wo_b, gf=g_f, tm=1024, name="out_b_prompt")
    (y_sample,) = _dense_call(xs1, og=obs, wo=wo_b, gf=g_f, name="out_b_sample")

    heads = lambda a, n, rows: a.reshape(n, rows, N_HEADS, HEAD_DIM)
    from_time_minor = lambda a: jnp.transpose(a.reshape(a.shape[0], N_HEADS, HEAD_DIM, a.shape[2]), (0, 3, 1, 2))
    return (y_prompt, y_sample.reshape(bs, ts, d),
            from_time_minor(akp)[None], from_time_minor(avp)[None],
            from_time_minor(kbp32), from_time_minor(vbp32),
            heads(aks, bs, ts)[None], heads(avs, bs, ts)[None],
            heads(kbs32, bs, ts), heads(vbs32, bs, ts))
```
